```python
import math
import jax, jax.numpy as jnp
from jax import lax
import numpy as np


D_MODEL = 1024
BATCH = 2
SEQ = 8192
DEPTH = 2

CHUNK = 64
Q_BLOCK = 128
EPS = 1e-6
NEG = -1e30

FOX_HEADS = 4
FOX_DIM = 64
DIFF_HEADS = 4
DIFF_QK = 64
DIFF_V = 2 * DIFF_QK
MLA_HEADS = 4
MLA_NOPE = 64
MLA_ROPE = 32
MLA_V = 64
MLA_Q_RANK = 256
MLA_KV_RANK = 128
ROPE_THETA = 10000.0
D_FF = 2816
CONV_W = 3

MIX_WIDTH = FOX_HEADS * FOX_DIM + DIFF_HEADS * DIFF_V + MLA_HEADS * MLA_V
IN_SIZES = (FOX_HEADS * FOX_DIM, FOX_HEADS * FOX_DIM, FOX_HEADS * FOX_DIM, FOX_HEADS,
            DIFF_HEADS * 2 * DIFF_QK, DIFF_HEADS * 2 * DIFF_QK, DIFF_HEADS * DIFF_V,
            MLA_Q_RANK, MLA_KV_RANK, MLA_ROPE)
IN_COLS = sum(IN_SIZES)

kernel_name = 'hybrid_fox_diff_mla_convffn'


def rmsnorm(x, g):
    xf = x.astype(jnp.float32)
    y = xf * lax.rsqrt(jnp.mean(xf * xf, axis=-1, keepdims=True) + EPS)
    return (y * g.astype(jnp.float32)).astype(x.dtype)


def to_heads(t, h):
    b, s, _ = t.shape
    return t.reshape(b, s, h, -1).transpose(0, 2, 1, 3)


def merge_heads(t):
    b, h, s, d = t.shape
    return t.transpose(0, 2, 1, 3).reshape(b, s, h * d)


def rope(x, pos):
    half = x.shape[-1] // 2
    inv = 1.0 / (ROPE_THETA ** (jnp.arange(half, dtype=jnp.float32) / half))
    ang = pos[:, None] * inv[None, :]
    cos, sin = jnp.cos(ang), jnp.sin(ang)
    xf = x.astype(jnp.float32)
    x1, x2 = xf[..., :half], xf[..., half:]
    return jnp.concatenate([x1 * cos - x2 * sin, x1 * sin + x2 * cos], axis=-1).astype(x.dtype)


def block_attention(q, k, v, bias_fn, scale):
    b, h, s, _ = q.shape
    dv = v.shape[-1]
    ki = jnp.arange(s)

    def one(blk):
        start = blk * Q_BLOCK
        qb = lax.dynamic_slice_in_dim(q, start, Q_BLOCK, axis=2)
        qi = start + jnp.arange(Q_BLOCK)
        sc = jnp.einsum('bhqd,bhkd->bhqk', qb, k).astype(jnp.float32) * scale
        p = jax.nn.softmax(sc + bias_fn(qi, ki), axis=-1).astype(v.dtype)
        return jnp.einsum('bhqk,bhkd->bhqd', p, v)

    out = lax.map(one, jnp.arange(s // Q_BLOCK))
    return out.transpose(1, 2, 0, 3, 4).reshape(b, h, s, dv)


def chunk_mask(qi, ki):
    return (ki[None, :] // CHUNK) <= (qi[:, None] // CHUNK)


def fox_group(q, k, v, f_logit, f_bias):
    q, k, v = to_heads(q, FOX_HEADS), to_heads(k, FOX_HEADS), to_heads(v, FOX_HEADS)
    logf = jax.nn.log_sigmoid((f_logit + f_bias).astype(jnp.float32))
    cum = jnp.cumsum(logf, axis=1).transpose(0, 2, 1)

    def bias(qi, ki):
        cq = jnp.take(cum, qi, axis=2)
        d = cq[..., :, None] - cum[..., None, :]
        return jnp.where((ki[None, :] <= qi[:, None])[None, None], d, NEG)

    return merge_heads(block_attention(q, k, v, bias, FOX_DIM ** -0.5))


def diff_group(q, k, v, lq1, lk1, lq2, lk2, norm_g, lam_init):
    q, k, v = to_heads(q, DIFF_HEADS), to_heads(k, DIFF_HEADS), to_heads(v, DIFF_HEADS)
    q1, q2 = q[..., :DIFF_QK], q[..., DIFF_QK:]
    k1, k2 = k[..., :DIFF_QK], k[..., DIFF_QK:]
    slopes = 2.0 ** (-8.0 * jnp.arange(1, DIFF_HEADS + 1, dtype=jnp.float32) / DIFF_HEADS)

    def bias(qi, ki):
        dist = jnp.abs(qi[:, None] - ki[None, :]).astype(jnp.float32)
        alibi = -slopes[:, None, None] * dist[None]
        return jnp.where(chunk_mask(qi, ki)[None], alibi, NEG)[None]

    scale = DIFF_QK ** -0.5
    o1 = block_attention(q1, k1, v, bias, scale)
    o2 = block_attention(q2, k2, v, bias, scale)
    lam = (jnp.exp(jnp.sum(lq1.astype(jnp.float32) * lk1.astype(jnp.float32)))
           - jnp.exp(jnp.sum(lq2.astype(jnp.float32) * lk2.astype(jnp.float32))) + lam_init)
    o = o1 - lam.astype(o1.dtype) * o2
    o = rmsnorm(o, norm_g) * (1.0 - lam_init)
    return merge_heads(o)


def mla_group(c_q, c_kv, k_rope_raw, q_norm_g, w_uq, kv_norm_g, w_ukv, pos):
    q = to_heads(rmsnorm(c_q, q_norm_g) @ w_uq, MLA_HEADS)
    q = jnp.concatenate([q[..., :MLA_NOPE], rope(q[..., MLA_NOPE:], pos)], axis=-1)
    kv = to_heads(rmsnorm(c_kv, kv_norm_g) @ w_ukv, MLA_HEADS)
    k_nope, v = kv[..., :MLA_NOPE], kv[..., MLA_NOPE:]
    k_r = rope(k_rope_raw[:, None], pos)
    k = jnp.concatenate([k_nope, jnp.broadcast_to(k_r, k_nope.shape[:3] + (MLA_ROPE,))], axis=-1)

    def bias(qi, ki):
        return jnp.where(chunk_mask(qi, ki), 0.0, NEG)[None, None]

    return merge_heads(block_attention(q, k, v, bias, (MLA_NOPE + MLA_ROPE) ** -0.5))


def causal_dwconv(u, w, b):
    s = u.shape[1]
    up = jnp.pad(u, ((0, 0), (CONV_W - 1, 0), (0, 0)))
    y = b
    for j in range(CONV_W):
        y = y + up[:, j:j + s, :] * w[j]
    return y


def setup_inputs(seed: int = 0) -> dict:
    key = jax.random.key(seed)
    ks = jax.random.split(key, 24)
    n = jax.random.normal
    f32 = jnp.float32
    return {
        'x': n(ks[0], (BATCH, SEQ, D_MODEL), f32),
        'ln1_g': 1.0 + 0.02 * n(ks[1], (DEPTH, D_MODEL), f32),
        'w_in': n(ks[2], (DEPTH, D_MODEL, IN_COLS), f32) * D_MODEL ** -0.5,
        'fgate_b': 2.0 + 0.5 * n(ks[3], (DEPTH, FOX_HEADS), f32),
        'lam_q1': 0.1 * n(ks[4], (DEPTH, DIFF_QK), f32),
        'lam_k1': 0.1 * n(ks[5], (DEPTH, DIFF_QK), f32),
        'lam_q2': 0.1 * n(ks[6], (DEPTH, DIFF_QK), f32),
        'lam_k2': 0.1 * n(ks[7], (DEPTH, DIFF_QK), f32),
        'diff_norm_g': 1.0 + 0.02 * n(ks[8], (DEPTH, DIFF_V), f32),
        'q_norm_g': 1.0 + 0.02 * n(ks[9], (DEPTH, MLA_Q_RANK), f32),
        'w_uq': n(ks[10], (DEPTH, MLA_Q_RANK, MLA_HEADS * (MLA_NOPE + MLA_ROPE)), f32) * MLA_Q_RANK ** -0.5,
        'kv_norm_g': 1.0 + 0.02 * n(ks[11], (DEPTH, MLA_KV_RANK), f32),
        'w_ukv': n(ks[12], (DEPTH, MLA_KV_RANK, MLA_HEADS * (MLA_NOPE + MLA_V)), f32) * MLA_KV_RANK ** -0.5,
        'w_o': n(ks[13], (DEPTH, MIX_WIDTH, D_MODEL), f32) * MIX_WIDTH ** -0.5,
        'ln2_g': 1.0 + 0.02 * n(ks[14], (DEPTH, D_MODEL), f32),
        'w_up': n(ks[15], (DEPTH, D_MODEL, 2 * D_FF), f32) * D_MODEL ** -0.5,
        'conv_w': n(ks[16], (DEPTH, CONV_W, 2 * D_FF), f32) * CONV_W ** -0.5,
        'conv_b': 0.02 * n(ks[17], (DEPTH, 2 * D_FF), f32),
        'w_down': n(ks[18], (DEPTH, D_FF, D_MODEL), f32) * D_FF ** -0.5,
        'final_g': 1.0 + 0.02 * n(ks[19], (D_MODEL,), f32),
    }


def reference(x, ln1_g, w_in, fgate_b, lam_q1, lam_k1, lam_q2, lam_k2, diff_norm_g,
              q_norm_g, w_uq, kv_norm_g, w_ukv, w_o, ln2_g, w_up, conv_w, conv_b,
              w_down, final_g):
    pos = jnp.arange(x.shape[1], dtype=jnp.float32)
    split_points = np.cumsum(IN_SIZES)[:-1].tolist()
    for i in range(DEPTH):
        lam_init = 0.8 - 0.6 * math.exp(-0.3 * i)
        h = rmsnorm(x, ln1_g[i]) @ w_in[i]
        (fq, fk, fv, ff, dq, dk, dv, c_q, c_kv, k_r) = jnp.split(h, split_points, axis=-1)
        o_fox = fox_group(fq, fk, fv, ff, fgate_b[i])
        o_diff = diff_group(dq, dk, dv, lam_q1[i], lam_k1[i], lam_q2[i], lam_k2[i],
                            diff_norm_g[i], lam_init)
        o_mla = mla_group(c_q, c_kv, k_r, q_norm_g[i], w_uq[i], kv_norm_g[i], w_ukv[i], pos)
        x = x + jnp.concatenate([o_fox, o_diff, o_mla], axis=-1) @ w_o[i]
        up = causal_dwconv(rmsnorm(x, ln2_g[i]) @ w_up[i], conv_w[i], conv_b[i])
        g, u = up[..., :D_FF], up[..., D_FF:]
        x = x + (jax.nn.silu(g) * u) @ w_down[i]
    return rmsnorm(x, final_g)
```

```python
import functools
import math

import jax
import jax.numpy as jnp
from jax import lax
from jax.experimental import pallas as pl
from jax.experimental.pallas import tpu as pltpu

F32 = jnp.float32
BF16 = jnp.bfloat16

D_MODEL = 1024
DEPTH = 2
CHUNK = 64
EPS = 1e-6
MASKED = -1e30

HEADS = 4
HEAD_PAD = 128
QK_DIM = 64
MLA_ROPE = 32
MLA_Q_RANK = 256
MLA_KV_RANK = 128
ROPE_THETA = 10000.0
D_FF = 2816
CONV_W = 3
BIAS_LANE = QK_DIM
ONES_LANE = QK_DIM

LOG2E = 1.4426950408889634

_GROUP = HEADS * HEAD_PAD
OFF_FQ, OFF_FK, OFF_FV = 0, _GROUP, 2 * _GROUP
OFF_FF = 3 * _GROUP
OFF_DQ1 = OFF_FF + 128
OFF_DQ2 = OFF_DQ1 + _GROUP
OFF_DK1 = OFF_DQ2 + _GROUP
OFF_DK2 = OFF_DK1 + _GROUP
OFF_DV = OFF_DK2 + _GROUP
OFF_CQ = OFF_DV + _GROUP
OFF_CKV = OFF_CQ + MLA_Q_RANK
OFF_KR = OFF_CKV + MLA_KV_RANK
OFF_KRP = OFF_KR + 128
PROJ_COLS = OFF_KRP + 128

TILE_PROJ = 512
TILE_ATTN = 512
TILE_FFN = 512
FF_CHUNK = 256
HALO = 16
VMEM_LIMIT = 56 * 1024 * 1024


def _dot(a, b):
    return jnp.dot(a, b, preferred_element_type=F32)


def _dot_nt(a, b):
    return lax.dot_general(a, b, (((1,), (1,)), ((), ())), preferred_element_type=F32)


def _rms(x, g):
    return x * lax.rsqrt(jnp.mean(x * x, axis=-1, keepdims=True) + EPS) * g


def _split3(x):
    hi = x.astype(BF16)
    r1 = x - hi.astype(F32)
    mid = r1.astype(BF16)
    lo = (r1 - mid.astype(F32)).astype(BF16)
    return hi, mid, lo


def _const_spec(shape):
    nd = len(shape)
    return pl.BlockSpec(shape, lambda *_: (0,) * nd, pipeline_mode=pl.Buffered(1))


def _proj_kernel(x_ref, g1_ref, wa_ref, fb_ref, qng_ref, wuq_ref, kvng_ref, wukv_ref,
                 cos_ref, sin_ref,
                 fq_ref, fk_ref, fv_ref, dq1_ref, dq2_ref, dk1_ref, dk2_ref, dv_ref,
                 mq_ref, mk_ref, mv_ref, carry_ref, *, tm):
    i = pl.program_id(1)

    @pl.when(i == 0)
    def _():
        carry_ref[...] = jnp.zeros_like(carry_ref)

    xn = _rms(x_ref[0], g1_ref[...]).astype(BF16)

    def proj(off, width):
        return _dot(xn, wa_ref[:, off:off + width])

    def head(t, h):
        return t[:, h * HEAD_PAD:(h + 1) * HEAD_PAD]

    lane = lax.broadcasted_iota(jnp.int32, (tm, HEAD_PAD), 1)
    ones3 = ((lane >= BIAS_LANE) & (lane < BIAS_LANE + 3)).astype(F32)
    one1 = (lane == ONES_LANE).astype(F32)
    qk_scale = QK_DIM ** -0.5 * LOG2E

    z = proj(OFF_FF, 128) + fb_ref[...]
    logf = jnp.minimum(z, 0.0) - jnp.log1p(jnp.exp(-jnp.abs(z)))
    rr = lax.broadcasted_iota(jnp.int32, (tm, tm), 0)
    cc = lax.broadcasted_iota(jnp.int32, (tm, tm), 1)
    tri = (cc <= rr).astype(BF16)
    l_hi, l_mid, l_lo = _split3(logf)
    cum = _dot(tri, l_hi) + _dot(tri, l_mid) + _dot(tri, l_lo) + carry_ref[...]
    carry_ref[...] = cum[tm - 1:tm, :]
    b_hi, b_mid, b_lo = _split3(cum * (-LOG2E))
    pr = lax.broadcasted_iota(jnp.int32, (128, _GROUP), 0)
    pc = lax.broadcasted_iota(jnp.int32, (128, _GROUP), 1)
    place = pc - pr * HEAD_PAD - BIAS_LANE
    fk_bias = (_dot(b_hi, (place == 0).astype(BF16)) + _dot(b_mid, (place == 1).astype(BF16))
               + _dot(b_lo, (place == 2).astype(BF16)))
    hq = proj(OFF_FQ, _GROUP)
    hk = proj(OFF_FK, _GROUP) + fk_bias
    hv = proj(OFF_FV, _GROUP)
    for h in range(HEADS):
        fq_ref[0, h] = (head(hq, h) * qk_scale + ones3).astype(BF16)
        fk_ref[0, h] = head(hk, h).astype(BF16)
        fv_ref[0, h] = (head(hv, h) + one1).astype(BF16)

    pos = (i * tm + lax.broadcasted_iota(jnp.int32, (tm, HEAD_PAD), 0)).astype(F32)
    hq1, hq2 = proj(OFF_DQ1, _GROUP), proj(OFF_DQ2, _GROUP)
    hk1, hk2 = proj(OFF_DK1, _GROUP), proj(OFF_DK2, _GROUP)
    hv = proj(OFF_DV, _GROUP)
    for h in range(HEADS):
        slope = 2.0 ** (-8.0 * (h + 1) / HEADS)
        a_hi, a_mid, a_lo = _split3(pos * (slope * LOG2E))
        kbias = jnp.where(lane == BIAS_LANE, a_hi.astype(F32),
                          jnp.where(lane == BIAS_LANE + 1, a_mid.astype(F32),
                                    jnp.where(lane == BIAS_LANE + 2, a_lo.astype(F32), 0.0)))
        dq1_ref[0, h] = (head(hq1, h) * qk_scale + ones3).astype(BF16)
        dq2_ref[0, h] = (head(hq2, h) * qk_scale + ones3).astype(BF16)
        dk1_ref[0, h] = (head(hk1, h) + kbias).astype(BF16)
        dk2_ref[0, h] = (head(hk2, h) + kbias).astype(BF16)
        dv_ref[0, h] = head(hv, h).astype(BF16)

    cqn = _rms(proj(OFF_CQ, MLA_Q_RANK), qng_ref[...]).astype(BF16)
    ckvn = _rms(proj(OFF_CKV, MLA_KV_RANK), kvng_ref[...]).astype(BF16)
    cos = cos_ref[...]
    sin = sin_ref[...]
    krot = proj(OFF_KR, 128) * cos + proj(OFF_KRP, 128) * sin
    qa = _dot(cqn, wuq_ref[:, :_GROUP])
    qb = _dot(cqn, wuq_ref[:, _GROUP:])
    kk = _dot(ckvn, wukv_ref[:, :_GROUP])
    vv = _dot(ckvn, wukv_ref[:, _GROUP:])
    mla_scale = (QK_DIM + MLA_ROPE) ** -0.5 * LOG2E
    for h in range(HEADS):
        mq_ref[0, h] = ((head(qa, h) * cos + head(qb, h) * sin) * mla_scale).astype(BF16)
        mk_ref[0, h] = (head(kk, h) + krot).astype(BF16)
        mv_ref[0, h] = (head(vv, h) + one1).astype(BF16)


def _project(x, g1, wa, fb, qng, wuq, kvng, wukv, cos_t, sin_t):
    b, s, _ = x.shape
    tm = TILE_PROJ
    head_out = jax.ShapeDtypeStruct((b, HEADS, s, HEAD_PAD), BF16)
    head_spec = pl.BlockSpec((1, HEADS, tm, HEAD_PAD), lambda bi, i: (bi, 0, i, 0))
    row_tab = pl.BlockSpec((tm, HEAD_PAD), lambda bi, i: (i, 0))
    return pl.pallas_call(
        functools.partial(_proj_kernel, tm=tm),
        grid=(b, s // tm),
        in_specs=[
            pl.BlockSpec((1, tm, D_MODEL), lambda bi, i: (bi, i, 0)),
            _const_spec((1, D_MODEL)),
            _const_spec((D_MODEL, PROJ_COLS)),
            _const_spec((1, 128)),
            _const_spec((1, MLA_Q_RANK)),
            _const_spec((MLA_Q_RANK, 2 * _GROUP)),
            _const_spec((1, MLA_KV_RANK)),
            _const_spec((MLA_KV_RANK, 2 * _GROUP)),
            row_tab, row_tab,
        ],
        out_specs=[head_spec] * 11,
        out_shape=[head_out] * 11,
        scratch_shapes=[pltpu.VMEM((1, 128), F32)],
        compiler_params=pltpu.CompilerParams(
            dimension_semantics=("arbitrary", "arbitrary"), vmem_limit_bytes=VMEM_LIMIT),
        name="proj",
    )(x, g1, wa, fb, qng, wuq, kvng, wukv, cos_t, sin_t)


def _tile_ids(t):
    row = lax.broadcasted_iota(jnp.int32, (t, t), 0)
    col = lax.broadcasted_iota(jnp.int32, (t, t), 1)
    return row, col


def _online_step(s, v, m_ref, acc_ref, l_ref=None):
    m_prev = m_ref[...]
    m_new = jnp.maximum(m_prev, jnp.max(s, axis=-1, keepdims=True))
    p = jnp.exp2(s - m_new)
    alpha = jnp.exp2(m_prev - m_new)
    if l_ref is not None:
        l_ref[...] = alpha * l_ref[...] + jnp.sum(p, axis=-1, keepdims=True)
    acc_ref[...] = alpha * acc_ref[...] + _dot(p.astype(BF16), v)
    m_ref[...] = m_new


def _single_map_kernel(q_ref, k_ref, v_ref, o_ref, m_ref, acc_ref, *, t, chunked):
    i = pl.program_id(2)
    q = q_ref[0, 0]
    m_ref[...] = jnp.full_like(m_ref, MASKED)
    acc_ref[...] = jnp.zeros_like(acc_ref)

    def tile(kt):
        ks = pl.multiple_of(kt * t, t)
        return k_ref[0, 0, pl.ds(ks, t), :], v_ref[0, 0, pl.ds(ks, t), :]

    def body(kt, c):
        k, v = tile(kt)
        _online_step(_dot_nt(q, k), v, m_ref, acc_ref)
        return c

    lax.fori_loop(0, i, body, 0)

    k, v = tile(i)
    row, col = _tile_ids(t)
    if chunked:
        keep = (col // CHUNK) <= (row // CHUNK)
    else:
        keep = col <= row
    _online_step(jnp.where(keep, _dot_nt(q, k), MASKED), v, m_ref, acc_ref)

    acc = acc_ref[...]
    o_ref[0] = (acc / acc[:, ONES_LANE:ONES_LANE + 1]).astype(BF16)


def _diff_kernel(q1_ref, q2_ref, k1_ref, k2_ref, v_ref, lam_ref, g_ref, o_ref,
                 m1_ref, l1_ref, acc1_ref, m2_ref, l2_ref, acc2_ref, *, t, lam_init):
    h = pl.program_id(1)
    i = pl.program_id(2)
    q1 = q1_ref[0, 0]
    q2 = q2_ref[0, 0]
    for m_ref, l_ref, acc_ref in ((m1_ref, l1_ref, acc1_ref), (m2_ref, l2_ref, acc2_ref)):
        m_ref[...] = jnp.full_like(m_ref, MASKED)
        l_ref[...] = jnp.zeros_like(l_ref)
        acc_ref[...] = jnp.zeros_like(acc_ref)

    def tile(kt):
        ks = pl.multiple_of(kt * t, t)
        sl = pl.ds(ks, t)
        return k1_ref[0, 0, sl, :], k2_ref[0, 0, sl, :], v_ref[0, 0, sl, :]

    def body(kt, c):
        k1, k2, v = tile(kt)
        _online_step(_dot_nt(q1, k1), v, m1_ref, acc1_ref, l1_ref)
        _online_step(_dot_nt(q2, k2), v, m2_ref, acc2_ref, l2_ref)
        return c

    lax.fori_loop(0, i, body, 0)

    k1, k2, v = tile(i)
    row, col = _tile_ids(t)
    keep = (col // CHUNK) <= (row // CHUNK)
    slope2 = jnp.exp2(jnp.full((1, 1), -2.0, F32) * (h + 1).astype(F32)) * (2.0 * LOG2E)
    fix = jnp.maximum(col - row, 0).astype(F32) * slope2
    _online_step(jnp.where(keep, _dot_nt(q1, k1) - fix, MASKED), v, m1_ref, acc1_ref, l1_ref)
    _online_step(jnp.where(keep, _dot_nt(q2, k2) - fix, MASKED), v, m2_ref, acc2_ref, l2_ref)

    lams = lam_ref[...]
    lam = (jnp.exp(jnp.sum(lams[0:1] * lams[1:2], axis=-1, keepdims=True))
           - jnp.exp(jnp.sum(lams[2:3] * lams[3:4], axis=-1, keepdims=True)) + lam_init)
    o = acc1_ref[...] / l1_ref[...] - lam * (acc2_ref[...] / l2_ref[...])
    o_ref[0] = (_rms(o, g_ref[...]) * (1.0 - lam_init)).astype(BF16)


def _attn_specs(s, t):
    q_spec = pl.BlockSpec((1, 1, t, HEAD_PAD), lambda b, h, i: (b, h, i, 0))
    kv_spec = pl.BlockSpec((1, 1, s, HEAD_PAD), lambda b, h, i: (b, h, 0, 0))
    o_spec = pl.BlockSpec((1, t, HEAD_PAD), lambda b, h, i: (b, i, h))
    return q_spec, kv_spec, o_spec


def _single_map_attention(q, k, v, *, chunked, name):
    b, _, s, _ = q.shape
    t = TILE_ATTN
    q_spec, kv_spec, o_spec = _attn_specs(s, t)
    return pl.pallas_call(
        functools.partial(_single_map_kernel, t=t, chunked=chunked),
        grid=(b, HEADS, s // t),
        in_specs=[q_spec, kv_spec, kv_spec],
        out_specs=o_spec,
        out_shape=jax.ShapeDtypeStruct((b, s, _GROUP), BF16),
        scratch_shapes=[pltpu.VMEM((t, 1), F32), pltpu.VMEM((t, HEAD_PAD), F32)],
        compiler_params=pltpu.CompilerParams(
            dimension_semantics=("arbitrary",) * 3, vmem_limit_bytes=VMEM_LIMIT),
        name=name,
    )(q, k, v)


def _diff_attention(q1, q2, k1, k2, v, lams, g, lam_init):
    b, _, s, _ = q1.shape
    t = TILE_ATTN
    q_spec, kv_spec, o_spec = _attn_specs(s, t)
    stat = pltpu.VMEM((t, 1), F32)
    acc = pltpu.VMEM((t, HEAD_PAD), F32)
    return pl.pallas_call(
        functools.partial(_diff_kernel, t=t, lam_init=lam_init),
        grid=(b, HEADS, s // t),
        in_specs=[q_spec, q_spec, kv_spec, kv_spec, kv_spec,
                  pl.BlockSpec((4, QK_DIM), lambda b, h, i: (0, 0)),
                  pl.BlockSpec((1, HEAD_PAD), lambda b, h, i: (0, 0))],
        out_specs=o_spec,
        out_shape=jax.ShapeDtypeStruct((b, s, _GROUP), BF16),
        scratch_shapes=[stat, stat, acc, stat, stat, acc],
        compiler_params=pltpu.CompilerParams(
            dimension_semantics=("arbitrary",) * 3, vmem_limit_bytes=VMEM_LIMIT),
        name="diff_attn",
    )(q1, q2, k1, k2, v, lams, g)


def _ffn_kernel(x_ref, of_ref, od_ref, om_ref, wof_ref, wod_ref, wom_ref, g2_ref,
                wup_ref, cw_ref, cb_ref, wdn_ref, gf_ref, o_ref, xn_ref, act_ref,
                *, tm, final):
    i = pl.program_id(1)

    @pl.when(i == 0)
    def _():
        xn_ref[0:HALO, :] = jnp.zeros((HALO, D_MODEL), BF16)

    @pl.when(i > 0)
    def _():
        xn_ref[0:HALO, :] = xn_ref[tm:tm + HALO, :]

    x1 = (x_ref[0] + _dot(of_ref[0], wof_ref[...]) + _dot(od_ref[0], wod_ref[...])
          + _dot(om_ref[0], wom_ref[...]))
    xn_ref[HALO:HALO + tm, :] = _rms(x1, g2_ref[...]).astype(BF16)

    def conv(up, off):
        cw = cw_ref[:, off:off + FF_CHUNK]
        y = cb_ref[:, off:off + FF_CHUNK]
        for j in range(CONV_W):
            lo = HALO - (CONV_W - 1) + j
            y = y + up[lo:lo + tm, :] * cw[j:j + 1, :]
        return y

    xn = xn_ref[...]
    for c in range(D_FF // FF_CHUNK):
        og = c * FF_CHUNK
        ou = D_FF + og
        yg = conv(_dot(xn, wup_ref[:, og:og + FF_CHUNK]), og)
        yu = conv(_dot(xn, wup_ref[:, ou:ou + FF_CHUNK]), ou)
        act = yg * (1.0 / (1.0 + jnp.exp(-yg))) * yu
        act_ref[:, og:og + FF_CHUNK] = act.astype(BF16)

    x2 = x1 + _dot(act_ref[...], wdn_ref[...])
    if final:
        x2 = _rms(x2, gf_ref[...])
    o_ref[0] = x2


def _ffn(x, o_fox, o_diff, o_mla, wo_f, wo_d, wo_m, g2, w_up, conv_w, conv_b, w_down, g_final,
         final):
    b, s, _ = x.shape
    tm = TILE_FFN
    row = lambda width: pl.BlockSpec((1, tm, width), lambda bi, i: (bi, i, 0))
    return pl.pallas_call(
        functools.partial(_ffn_kernel, tm=tm, final=final),
        grid=(b, s // tm),
        in_specs=[
            row(D_MODEL), row(_GROUP), row(_GROUP), row(_GROUP),
            _const_spec((_GROUP, D_MODEL)), _const_spec((_GROUP, D_MODEL)),
            _const_spec((_GROUP, D_MODEL)),
            _const_spec((1, D_MODEL)),
            _const_spec((D_MODEL, 2 * D_FF)),
            _const_spec((CONV_W, 2 * D_FF)),
            _const_spec((1, 2 * D_FF)),
            _const_spec((D_FF, D_MODEL)),
            _const_spec((1, D_MODEL)),
        ],
        out_specs=row(D_MODEL),
        out_shape=jax.ShapeDtypeStruct((b, s, D_MODEL), F32),
        scratch_shapes=[pltpu.VMEM((HALO + tm, D_MODEL), BF16), pltpu.VMEM((tm, D_FF), BF16)],
        compiler_params=pltpu.CompilerParams(
            dimension_semantics=("arbitrary", "arbitrary"), vmem_limit_bytes=VMEM_LIMIT),
        name="ffn",
    )(x, o_fox, o_diff, o_mla, wo_f, wo_d, wo_m, g2, w_up, conv_w, conv_b, w_down, g_final)


def _pad_heads(w, width):
    rows = w.shape[0]
    w = w.reshape(rows, HEADS, width)
    w = jnp.pad(w, ((0, 0), (0, 0), (0, HEAD_PAD - width)))
    return w.reshape(rows, HEADS * HEAD_PAD)


def _rot_half_cols(w):
    half = MLA_ROPE // 2
    return jnp.concatenate([w[..., half:], w[..., :half]], axis=-1)


def _at_rope_lanes(w):
    return jnp.pad(w, ((0, 0), (QK_DIM, HEAD_PAD - QK_DIM - MLA_ROPE)))


def _layout_w_in(w):
    sizes = (256, 256, 256, 4, 512, 512, 512, MLA_Q_RANK, MLA_KV_RANK, MLA_ROPE)
    splits = [sum(sizes[:n]) for n in range(1, len(sizes))]
    fq, fk, fv, ff, dq, dk, dv, cq, ckv, kr = jnp.split(w, splits, axis=1)
    dq = dq.reshape(D_MODEL, HEADS, 2, QK_DIM)
    dk = dk.reshape(D_MODEL, HEADS, 2, QK_DIM)
    cols = [
        _pad_heads(fq, QK_DIM), _pad_heads(fk, QK_DIM), _pad_heads(fv, QK_DIM),
        jnp.pad(ff, ((0, 0), (0, 128 - HEADS))),
        _pad_heads(dq[:, :, 0].reshape(D_MODEL, -1), QK_DIM),
        _pad_heads(dq[:, :, 1].reshape(D_MODEL, -1), QK_DIM),
        _pad_heads(dk[:, :, 0].reshape(D_MODEL, -1), QK_DIM),
        _pad_heads(dk[:, :, 1].reshape(D_MODEL, -1), QK_DIM),
        dv, cq, ckv, _at_rope_lanes(kr), _at_rope_lanes(_rot_half_cols(kr)),
    ]
    out = jnp.concatenate(cols, axis=1).astype(BF16)
    assert out.shape == (D_MODEL, PROJ_COLS)
    return out


def _layout_w_uq(w):
    w = w.reshape(MLA_Q_RANK, HEADS, QK_DIM + MLA_ROPE)
    nope, rope = w[..., :QK_DIM], w[..., QK_DIM:]
    pad = jnp.zeros((MLA_Q_RANK, HEADS, HEAD_PAD - QK_DIM - MLA_ROPE), w.dtype)
    direct = jnp.concatenate([nope, rope, pad], axis=-1)
    swapped = jnp.concatenate([jnp.zeros_like(nope), _rot_half_cols(rope), pad], axis=-1)
    return jnp.concatenate([direct.reshape(MLA_Q_RANK, -1), swapped.reshape(MLA_Q_RANK, -1)],
                           axis=1).astype(BF16)


def _layout_w_ukv(w):
    w = w.reshape(MLA_KV_RANK, HEADS, 2 * QK_DIM)
    k = _pad_heads(w[..., :QK_DIM].reshape(MLA_KV_RANK, -1), QK_DIM)
    v = _pad_heads(w[..., QK_DIM:].reshape(MLA_KV_RANK, -1), QK_DIM)
    return jnp.concatenate([k, v], axis=1).astype(BF16)


def _pad_head_rows(w):
    w = w.reshape(HEADS, QK_DIM, -1)
    w = jnp.pad(w, ((0, 0), (0, HEAD_PAD - QK_DIM), (0, 0)))
    return w.reshape(HEADS * HEAD_PAD, -1)


def _rope_tables(s):
    half = MLA_ROPE // 2
    pos = jnp.arange(s, dtype=F32)
    inv = 1.0 / (ROPE_THETA ** (jnp.arange(half, dtype=F32) / half))
    ang = pos[:, None] * inv[None, :]
    cos, sin = jnp.cos(ang), jnp.sin(ang)
    tail = jnp.zeros((s, HEAD_PAD - QK_DIM - MLA_ROPE), F32)
    cos_t = jnp.concatenate([jnp.ones((s, QK_DIM), F32), cos, cos, tail], axis=1)
    sin_t = jnp.concatenate([jnp.zeros((s, QK_DIM), F32), -sin, sin, tail], axis=1)
    return cos_t, sin_t


def kernel(x, ln1_g, w_in, fgate_b, lam_q1, lam_k1, lam_q2, lam_k2, diff_norm_g, q_norm_g, w_uq,
           kv_norm_g, w_ukv, w_o, ln2_g, w_up, conv_w, conv_b, w_down, final_g):
    s = x.shape[1]
    cos_t, sin_t = _rope_tables(s)
    n_fox = HEADS * QK_DIM
    n_diff = HEADS * HEAD_PAD
    for i in range(DEPTH):
        lam_init = 0.8 - 0.6 * math.exp(-0.3 * i)
        fb = jnp.pad(fgate_b[i], (0, 128 - HEADS)).reshape(1, 128)
        heads = _project(x, ln1_g[i].reshape(1, -1), _layout_w_in(w_in[i]), fb,
                         q_norm_g[i].reshape(1, -1), _layout_w_uq(w_uq[i]),
                         kv_norm_g[i].reshape(1, -1), _layout_w_ukv(w_ukv[i]), cos_t, sin_t)
        fq, fk, fv, dq1, dq2, dk1, dk2, dv, mq, mk, mv = heads
        o_fox = _single_map_attention(fq, fk, fv, chunked=False, name="fox_attn")
        lams = jnp.stack([lam_q1[i], lam_k1[i], lam_q2[i], lam_k2[i]])
        o_diff = _diff_attention(dq1, dq2, dk1, dk2, dv, lams, diff_norm_g[i].reshape(1, -1),
                                 lam_init)
        o_mla = _single_map_attention(mq, mk, mv, chunked=True, name="mla_attn")
        wo = w_o[i].astype(BF16)
        x = _ffn(x, o_fox, o_diff, o_mla,
                 _pad_head_rows(wo[:n_fox]), wo[n_fox:n_fox + n_diff],
                 _pad_head_rows(wo[n_fox + n_diff:]),
                 ln2_g[i].reshape(1, -1), w_up[i].astype(BF16), conv_w[i],
                 conv_b[i].reshape(1, -1), w_down[i].astype(BF16), final_g.reshape(1, -1),
                 final=(i == DEPTH - 1))
    return x
```

```python
import functools
import math

import jax
import jax.numpy as jnp
from jax import lax
from jax.experimental import pallas as pl
from jax.experimental.pallas import tpu as pltpu

F32 = jnp.float32
BF16 = jnp.bfloat16

D_MODEL = 1024
DEPTH = 2
CHUNK = 64
EPS = 1e-6
MASKED = -1e30

HEADS = 4
HEAD_PAD = 128
QK_DIM = 64
MLA_ROPE = 32
MLA_Q_RANK = 256
MLA_KV_RANK = 128
ROPE_THETA = 10000.0
D_FF = 2816
CONV_W = 3
BIAS_LANE = QK_DIM
ONES_LANE = QK_DIM

LOG2E = 1.4426950408889634

_GROUP = HEADS * HEAD_PAD
OFF_FQ, OFF_FK, OFF_FV = 0, _GROUP, 2 * _GROUP
OFF_FF = 3 * _GROUP
OFF_DQ1 = OFF_FF + 128
OFF_DQ2 = OFF_DQ1 + _GROUP
OFF_DK1 = OFF_DQ2 + _GROUP
OFF_DK2 = OFF_DK1 + _GROUP
OFF_DV = OFF_DK2 + _GROUP
OFF_CQ = OFF_DV + _GROUP
OFF_CKV = OFF_CQ + MLA_Q_RANK
OFF_KR = OFF_CKV + MLA_KV_RANK
OFF_KRP = OFF_KR + 128
PROJ_COLS = OFF_KRP + 128

TILE_ATTN = 512
TILE_PROJ = TILE_ATTN
TILE_FFN = 512
FF_CHUNK = 256
HALO = 16
VMEM_LIMIT = 56 * 1024 * 1024


def _dot(a, b):
    return jnp.dot(a, b, preferred_element_type=F32)


def _dot_nt(a, b):
    return lax.dot_general(a, b, (((1,), (1,)), ((), ())), preferred_element_type=F32)


def _rms(x, g):
    return x * lax.rsqrt(jnp.mean(x * x, axis=-1, keepdims=True) + EPS) * g


def _split3(x):
    hi = x.astype(BF16)
    r1 = x - hi.astype(F32)
    mid = r1.astype(BF16)
    lo = (r1 - mid.astype(F32)).astype(BF16)
    return hi, mid, lo


def _const_spec(shape):
    nd = len(shape)
    return pl.BlockSpec(shape, lambda *_: (0,) * nd, pipeline_mode=pl.Buffered(1))


def _proj_kernel(x_ref, g1_ref, wa_ref, fb_ref, qng_ref, wuq_ref, kvng_ref, wukv_ref,
                 cos_ref, sin_ref,
                 fq_ref, fk_ref, fvt_ref, dq1_ref, dq2_ref, dk1_ref, dk2_ref, dvt_ref,
                 mq_ref, mk_ref, mvt_ref, carry_ref, *, tm):
    i = pl.program_id(1)

    @pl.when(i == 0)
    def _():
        carry_ref[...] = jnp.zeros_like(carry_ref)

    xn = _rms(x_ref[0], g1_ref[...]).astype(BF16)

    def proj(off, width):
        return _dot(xn, wa_ref[:, off:off + width])

    def head(t, h):
        return t[:, h * HEAD_PAD:(h + 1) * HEAD_PAD]

    lane = lax.broadcasted_iota(jnp.int32, (tm, HEAD_PAD), 1)
    ones3 = ((lane >= BIAS_LANE) & (lane < BIAS_LANE + 3)).astype(F32)
    one1 = (lane == ONES_LANE).astype(F32)
    qk_scale = QK_DIM ** -0.5 * LOG2E

    z = proj(OFF_FF, 128) + fb_ref[...]
    logf = jnp.minimum(z, 0.0) - jnp.log1p(jnp.exp(-jnp.abs(z)))
    rr = lax.broadcasted_iota(jnp.int32, (tm, tm), 0)
    cc = lax.broadcasted_iota(jnp.int32, (tm, tm), 1)
    tri = (cc <= rr).astype(BF16)
    l_hi, l_mid, l_lo = _split3(logf)
    cum = _dot(tri, l_hi) + _dot(tri, l_mid) + _dot(tri, l_lo) + carry_ref[...]
    carry_ref[...] = cum[tm - 1:tm, :]
    b_hi, b_mid, b_lo = _split3(cum * (-LOG2E))
    pr = lax.broadcasted_iota(jnp.int32, (128, _GROUP), 0)
    pc = lax.broadcasted_iota(jnp.int32, (128, _GROUP), 1)
    place = pc - pr * HEAD_PAD - BIAS_LANE
    fk_bias = (_dot(b_hi, (place == 0).astype(BF16)) + _dot(b_mid, (place == 1).astype(BF16))
               + _dot(b_lo, (place == 2).astype(BF16)))
    hq = proj(OFF_FQ, _GROUP)
    hk = proj(OFF_FK, _GROUP) + fk_bias
    hv = proj(OFF_FV, _GROUP)
    for h in range(HEADS):
        fq_ref[0, h] = (head(hq, h) * qk_scale + ones3).astype(BF16)
        fk_ref[0, h] = head(hk, h).astype(BF16)
        fvt_ref[0, h, 0] = (head(hv, h) + one1).T.astype(BF16)

    pos = (i * tm + lax.broadcasted_iota(jnp.int32, (tm, HEAD_PAD), 0)).astype(F32)
    hq1, hq2 = proj(OFF_DQ1, _GROUP), proj(OFF_DQ2, _GROUP)
    hk1, hk2 = proj(OFF_DK1, _GROUP), proj(OFF_DK2, _GROUP)
    hv = proj(OFF_DV, _GROUP)
    for h in range(HEADS):
        slope = 2.0 ** (-8.0 * (h + 1) / HEADS)
        a_hi, a_mid, a_lo = _split3(pos * (slope * LOG2E))
        kbias = jnp.where(lane == BIAS_LANE, a_hi.astype(F32),
                          jnp.where(lane == BIAS_LANE + 1, a_mid.astype(F32),
                                    jnp.where(lane == BIAS_LANE + 2, a_lo.astype(F32), 0.0)))
        dq1_ref[0, h] = (head(hq1, h) * qk_scale + ones3).astype(BF16)
        dq2_ref[0, h] = (head(hq2, h) * qk_scale + ones3).astype(BF16)
        dk1_ref[0, h] = (head(hk1, h) + kbias).astype(BF16)
        dk2_ref[0, h] = (head(hk2, h) + kbias).astype(BF16)
        dvt_ref[0, h, 0] = head(hv, h).T.astype(BF16)

    cqn = _rms(proj(OFF_CQ, MLA_Q_RANK), qng_ref[...]).astype(BF16)
    ckvn = _rms(proj(OFF_CKV, MLA_KV_RANK), kvng_ref[...]).astype(BF16)
    cos = cos_ref[...]
    sin = sin_ref[...]
    krot = proj(OFF_KR, 128) * cos + proj(OFF_KRP, 128) * sin
    qa = _dot(cqn, wuq_ref[:, :_GROUP])
    qb = _dot(cqn, wuq_ref[:, _GROUP:])
    kk = _dot(ckvn, wukv_ref[:, :_GROUP])
    vv = _dot(ckvn, wukv_ref[:, _GROUP:])
    mla_scale = (QK_DIM + MLA_ROPE) ** -0.5 * LOG2E
    for h in range(HEADS):
        mq_ref[0, h] = ((head(qa, h) * cos + head(qb, h) * sin) * mla_scale).astype(BF16)
        mk_ref[0, h] = (head(kk, h) + krot).astype(BF16)
        mvt_ref[0, h, 0] = (head(vv, h) + one1).T.astype(BF16)


def _project(x, g1, wa, fb, qng, wuq, kvng, wukv, cos_t, sin_t):
    b, s, _ = x.shape
    tm = TILE_PROJ
    rows = (jax.ShapeDtypeStruct((b, HEADS, s, HEAD_PAD), BF16),
            pl.BlockSpec((1, HEADS, tm, HEAD_PAD), lambda bi, i: (bi, 0, i, 0)))
    cols = (jax.ShapeDtypeStruct((b, HEADS, s // tm, HEAD_PAD, tm), BF16),
            pl.BlockSpec((1, HEADS, 1, HEAD_PAD, tm), lambda bi, i: (bi, 0, i, 0, 0)))
    outs = [rows, rows, cols, rows, rows, rows, rows, cols, rows, rows, cols]
    row_tab = pl.BlockSpec((tm, HEAD_PAD), lambda bi, i: (i, 0))
    return pl.pallas_call(
        functools.partial(_proj_kernel, tm=tm),
        grid=(b, s // tm),
        in_specs=[
            pl.BlockSpec((1, tm, D_MODEL), lambda bi, i: (bi, i, 0)),
            _const_spec((1, D_MODEL)),
            _const_spec((D_MODEL, PROJ_COLS)),
            _const_spec((1, 128)),
            _const_spec((1, MLA_Q_RANK)),
            _const_spec((MLA_Q_RANK, 2 * _GROUP)),
            _const_spec((1, MLA_KV_RANK)),
            _const_spec((MLA_KV_RANK, 2 * _GROUP)),
            row_tab, row_tab,
        ],
        out_specs=[o[1] for o in outs],
        out_shape=[o[0] for o in outs],
        scratch_shapes=[pltpu.VMEM((1, 128), F32)],
        compiler_params=pltpu.CompilerParams(
            dimension_semantics=("arbitrary", "arbitrary"), vmem_limit_bytes=VMEM_LIMIT),
        name="proj",
    )(x, g1, wa, fb, qng, wuq, kvng, wukv, cos_t, sin_t)


def _tile_ids(t):
    key = lax.broadcasted_iota(jnp.int32, (t, t), 0)
    qry = lax.broadcasted_iota(jnp.int32, (t, t), 1)
    return key, qry


def _online_step(s, vt, m_ref, acc_ref, l_ref=None):
    m_prev = m_ref[...]
    m_new = jnp.maximum(m_prev, jnp.max(s, axis=0, keepdims=True))
    p = jnp.exp2(s - m_new)
    alpha = jnp.exp2(m_prev - m_new)
    if l_ref is not None:
        l_ref[...] = alpha * l_ref[...] + jnp.sum(p, axis=0, keepdims=True)
    acc_ref[...] = alpha * acc_ref[...] + _dot(vt, p.astype(BF16))
    m_ref[...] = m_new


def _run_tiles(n_off, compute_scores, consume, consume_diag, buf_a, buf_b):
    odd = jnp.bitwise_and(n_off, 1)

    @pl.when(odd == 1)
    def _():
        compute_scores(0, buf_b)
        compute_scores(1, buf_a)
        consume(0, buf_b)

    @pl.when(odd == 0)
    def _():
        compute_scores(0, buf_a)

    def body(jj, c):
        j = odd + 2 * jj
        compute_scores(j + 1, buf_b)
        consume(j, buf_a)
        compute_scores(j + 2, buf_a)
        consume(j + 1, buf_b)
        return c

    lax.fori_loop(0, lax.shift_right_logical(n_off, 1), body, 0)
    consume_diag(n_off, buf_a)


def _single_map_kernel(q_ref, k_ref, vt_ref, o_ref, sa_ref, sb_ref, m_ref, acc_ref, *, t, chunked):
    i = pl.program_id(2)
    m_ref[...] = jnp.full_like(m_ref, MASKED)
    acc_ref[...] = jnp.zeros_like(acc_ref)

    def compute_scores(kt, buf):
        ks = pl.multiple_of(kt * t, t)
        buf[0][...] = _dot_nt(k_ref[0, 0, pl.ds(ks, t), :], q_ref[0, 0])

    def consume(kt, buf):
        _online_step(buf[0][...], vt_ref[0, 0, kt], m_ref, acc_ref)

    def consume_diag(kt, buf):
        key, qry = _tile_ids(t)
        keep = (key // CHUNK) <= (qry // CHUNK) if chunked else key <= qry
        _online_step(jnp.where(keep, buf[0][...], MASKED), vt_ref[0, 0, kt], m_ref, acc_ref)

    _run_tiles(i, compute_scores, consume, consume_diag, (sa_ref,), (sb_ref,))

    acc = acc_ref[...]
    o_t = acc * (1.0 / acc[ONES_LANE:ONES_LANE + 1, :])
    o_ref[0] = o_t.T.astype(BF16)


def _diff_kernel(q1_ref, q2_ref, k1_ref, k2_ref, vt_ref, lam_ref, g_ref, o_ref,
                 s1a_ref, s2a_ref, s1b_ref, s2b_ref,
                 m1_ref, l1_ref, acc1_ref, m2_ref, l2_ref, acc2_ref, *, t, lam_init):
    h = pl.program_id(1)
    i = pl.program_id(2)
    for m_ref, l_ref, acc_ref in ((m1_ref, l1_ref, acc1_ref), (m2_ref, l2_ref, acc2_ref)):
        m_ref[...] = jnp.full_like(m_ref, MASKED)
        l_ref[...] = jnp.zeros_like(l_ref)
        acc_ref[...] = jnp.zeros_like(acc_ref)

    def compute_scores(kt, buf):
        sl = pl.ds(pl.multiple_of(kt * t, t), t)
        buf[0][...] = _dot_nt(k1_ref[0, 0, sl, :], q1_ref[0, 0])
        buf[1][...] = _dot_nt(k2_ref[0, 0, sl, :], q2_ref[0, 0])

    def consume(kt, buf):
        vt = vt_ref[0, 0, kt]
        _online_step(buf[0][...], vt, m1_ref, acc1_ref, l1_ref)
        _online_step(buf[1][...], vt, m2_ref, acc2_ref, l2_ref)

    def consume_diag(kt, buf):
        key, qry = _tile_ids(t)
        keep = (key // CHUNK) <= (qry // CHUNK)
        slope2 = jnp.exp2(jnp.full((1, 1), -2.0, F32) * (h + 1).astype(F32)) * (2.0 * LOG2E)
        fix = jnp.maximum(key - qry, 0).astype(F32) * slope2
        vt = vt_ref[0, 0, kt]
        _online_step(jnp.where(keep, buf[0][...] - fix, MASKED), vt, m1_ref, acc1_ref, l1_ref)
        _online_step(jnp.where(keep, buf[1][...] - fix, MASKED), vt, m2_ref, acc2_ref, l2_ref)

    _run_tiles(i, compute_scores, consume, consume_diag,
               (s1a_ref, s2a_ref), (s1b_ref, s2b_ref))

    lams = lam_ref[...]
    lam = (jnp.exp(jnp.sum(lams[0:1] * lams[1:2], axis=-1, keepdims=True))
           - jnp.exp(jnp.sum(lams[2:3] * lams[3:4], axis=-1, keepdims=True)) + lam_init)
    o_t = acc1_ref[...] * (1.0 / l1_ref[...]) - lam * (acc2_ref[...] * (1.0 / l2_ref[...]))
    o_ref[0] = (_rms(o_t.T, g_ref[...]) * (1.0 - lam_init)).astype(BF16)


def _attn_specs(s, t):
    q_spec = pl.BlockSpec((1, 1, t, HEAD_PAD), lambda b, h, i: (b, h, i, 0))
    k_spec = pl.BlockSpec((1, 1, s, HEAD_PAD), lambda b, h, i: (b, h, 0, 0))
    vt_spec = pl.BlockSpec((1, 1, s // t, HEAD_PAD, t), lambda b, h, i: (b, h, 0, 0, 0))
    o_spec = pl.BlockSpec((1, t, HEAD_PAD), lambda b, h, i: (b, i, h))
    return q_spec, k_spec, vt_spec, o_spec


def _single_map_attention(q, k, vt, *, chunked, name):
    b, _, s, _ = q.shape
    t = TILE_ATTN
    q_spec, k_spec, vt_spec, o_spec = _attn_specs(s, t)
    score = pltpu.VMEM((t, t), F32)
    return pl.pallas_call(
        functools.partial(_single_map_kernel, t=t, chunked=chunked),
        grid=(b, HEADS, s // t),
        in_specs=[q_spec, k_spec, vt_spec],
        out_specs=o_spec,
        out_shape=jax.ShapeDtypeStruct((b, s, _GROUP), BF16),
        scratch_shapes=[score, score, pltpu.VMEM((1, t), F32), pltpu.VMEM((HEAD_PAD, t), F32)],
        compiler_params=pltpu.CompilerParams(
            dimension_semantics=("arbitrary",) * 3, vmem_limit_bytes=VMEM_LIMIT),
        name=name,
    )(q, k, vt)


def _diff_attention(q1, q2, k1, k2, vt, lams, g, lam_init):
    b, _, s, _ = q1.shape
    t = TILE_ATTN
    q_spec, k_spec, vt_spec, o_spec = _attn_specs(s, t)
    score = pltpu.VMEM((t, t), F32)
    stat = pltpu.VMEM((1, t), F32)
    acc = pltpu.VMEM((HEAD_PAD, t), F32)
    return pl.pallas_call(
        functools.partial(_diff_kernel, t=t, lam_init=lam_init),
        grid=(b, HEADS, s // t),
        in_specs=[q_spec, q_spec, k_spec, k_spec, vt_spec,
                  pl.BlockSpec((4, QK_DIM), lambda b, h, i: (0, 0)),
                  pl.BlockSpec((1, HEAD_PAD), lambda b, h, i: (0, 0))],
        out_specs=o_spec,
        out_shape=jax.ShapeDtypeStruct((b, s, _GROUP), BF16),
        scratch_shapes=[score, score, score, score, stat, stat, acc, stat, stat, acc],
        compiler_params=pltpu.CompilerParams(
            dimension_semantics=("arbitrary",) * 3, vmem_limit_bytes=VMEM_LIMIT),
        name="diff_attn",
    )(q1, q2, k1, k2, vt, lams, g)


def _ffn_kernel(x_ref, of_ref, od_ref, om_ref, wof_ref, wod_ref, wom_ref, g2_ref,
                wup_ref, cw_ref, cb_ref, wdn_ref, gf_ref, o_ref, xn_ref, act_ref,
                *, tm, final):
    i = pl.program_id(1)

    @pl.when(i == 0)
    def _():
        xn_ref[0:HALO, :] = jnp.zeros((HALO, D_MODEL), BF16)

    @pl.when(i > 0)
    def _():
        xn_ref[0:HALO, :] = xn_ref[tm:tm + HALO, :]

    x1 = (x_ref[0] + _dot(of_ref[0], wof_ref[...]) + _dot(od_ref[0], wod_ref[...])
          + _dot(om_ref[0], wom_ref[...]))
    xn_ref[HALO:HALO + tm, :] = _rms(x1, g2_ref[...]).astype(BF16)

    def conv(up, off):
        cw = cw_ref[:, off:off + FF_CHUNK]
        y = cb_ref[:, off:off + FF_CHUNK]
        for j in range(CONV_W):
            lo = HALO - (CONV_W - 1) + j
            y = y + up[lo:lo + tm, :] * cw[j:j + 1, :]
        return y

    xn = xn_ref[...]
    for c in range(D_FF // FF_CHUNK):
        og = c * FF_CHUNK
        ou = D_FF + og
        yg = conv(_dot(xn, wup_ref[:, og:og + FF_CHUNK]), og)
        yu = conv(_dot(xn, wup_ref[:, ou:ou + FF_CHUNK]), ou)
        act = yg * (1.0 / (1.0 + jnp.exp(-yg))) * yu
        act_ref[:, og:og + FF_CHUNK] = act.astype(BF16)

    x2 = x1 + _dot(act_ref[...], wdn_ref[...])
    if final:
        x2 = _rms(x2, gf_ref[...])
    o_ref[0] = x2


def _ffn(x, o_fox, o_diff, o_mla, wo_f, wo_d, wo_m, g2, w_up, conv_w, conv_b, w_down, g_final,
         final):
    b, s, _ = x.shape
    tm = TILE_FFN
    row = lambda width: pl.BlockSpec((1, tm, width), lambda bi, i: (bi, i, 0))
    return pl.pallas_call(
        functools.partial(_ffn_kernel, tm=tm, final=final),
        grid=(b, s // tm),
        in_specs=[
            row(D_MODEL), row(_GROUP), row(_GROUP), row(_GROUP),
            _const_spec((_GROUP, D_MODEL)), _const_spec((_GROUP, D_MODEL)),
            _const_spec((_GROUP, D_MODEL)),
            _const_spec((1, D_MODEL)),
            _const_spec((D_MODEL, 2 * D_FF)),
            _const_spec((CONV_W, 2 * D_FF)),
            _const_spec((1, 2 * D_FF)),
            _const_spec((D_FF, D_MODEL)),
            _const_spec((1, D_MODEL)),
        ],
        out_specs=row(D_MODEL),
        out_shape=jax.ShapeDtypeStruct((b, s, D_MODEL), F32),
        scratch_shapes=[pltpu.VMEM((HALO + tm, D_MODEL), BF16), pltpu.VMEM((tm, D_FF), BF16)],
        compiler_params=pltpu.CompilerParams(
            dimension_semantics=("arbitrary", "arbitrary"), vmem_limit_bytes=VMEM_LIMIT),
        name="ffn",
    )(x, o_fox, o_diff, o_mla, wo_f, wo_d, wo_m, g2, w_up, conv_w, conv_b, w_down, g_final)


def _pad_heads(w, width):
    rows = w.shape[0]
    w = w.reshape(rows, HEADS, width)
    w = jnp.pad(w, ((0, 0), (0, 0), (0, HEAD_PAD - width)))
    return w.reshape(rows, HEADS * HEAD_PAD)


def _rot_half_cols(w):
    half = MLA_ROPE // 2
    return jnp.concatenate([w[..., half:], w[..., :half]], axis=-1)


def _at_rope_lanes(w):
    return jnp.pad(w, ((0, 0), (QK_DIM, HEAD_PAD - QK_DIM - MLA_ROPE)))


def _layout_w_in(w):
    sizes = (256, 256, 256, 4, 512, 512, 512, MLA_Q_RANK, MLA_KV_RANK, MLA_ROPE)
    splits = [sum(sizes[:n]) for n in range(1, len(sizes))]
    fq, fk, fv, ff, dq, dk, dv, cq, ckv, kr = jnp.split(w, splits, axis=1)
    dq = dq.reshape(D_MODEL, HEADS, 2, QK_DIM)
    dk = dk.reshape(D_MODEL, HEADS, 2, QK_DIM)
    cols = [
        _pad_heads(fq, QK_DIM), _pad_heads(fk, QK_DIM), _pad_heads(fv, QK_DIM),
        jnp.pad(ff, ((0, 0), (0, 128 - HEADS))),
        _pad_heads(dq[:, :, 0].reshape(D_MODEL, -1), QK_DIM),
        _pad_heads(dq[:, :, 1].reshape(D_MODEL, -1), QK_DIM),
        _pad_heads(dk[:, :, 0].reshape(D_MODEL, -1), QK_DIM),
        _pad_heads(dk[:, :, 1].reshape(D_MODEL, -1), QK_DIM),
        dv, cq, ckv, _at_rope_lanes(kr), _at_rope_lanes(_rot_half_cols(kr)),
    ]
    out = jnp.concatenate(cols, axis=1).astype(BF16)
    assert out.shape == (D_MODEL, PROJ_COLS)
    return out


def _layout_w_uq(w):
    w = w.reshape(MLA_Q_RANK, HEADS, QK_DIM + MLA_ROPE)
    nope, rope = w[..., :QK_DIM], w[..., QK_DIM:]
    pad = jnp.zeros((MLA_Q_RANK, HEADS, HEAD_PAD - QK_DIM - MLA_ROPE), w.dtype)
    direct = jnp.concatenate([nope, rope, pad], axis=-1)
    swapped = jnp.concatenate([jnp.zeros_like(nope), _rot_half_cols(rope), pad], axis=-1)
    return jnp.concatenate([direct.reshape(MLA_Q_RANK, -1), swapped.reshape(MLA_Q_RANK, -1)],
                           axis=1).astype(BF16)


def _layout_w_ukv(w):
    w = w.reshape(MLA_KV_RANK, HEADS, 2 * QK_DIM)
    k = _pad_heads(w[..., :QK_DIM].reshape(MLA_KV_RANK, -1), QK_DIM)
    v = _pad_heads(w[..., QK_DIM:].reshape(MLA_KV_RANK, -1), QK_DIM)
    return jnp.concatenate([k, v], axis=1).astype(BF16)


def _pad_head_rows(w):
    w = w.reshape(HEADS, QK_DIM, -1)
    w = jnp.pad(w, ((0, 0), (0, HEAD_PAD - QK_DIM), (0, 0)))
    return w.reshape(HEADS * HEAD_PAD, -1)


def _rope_tables(s):
    half = MLA_ROPE // 2
    pos = jnp.arange(s, dtype=F32)
    inv = 1.0 / (ROPE_THETA ** (jnp.arange(half, dtype=F32) / half))
    ang = pos[:, None] * inv[None, :]
    cos, sin = jnp.cos(ang), jnp.sin(ang)
    tail = jnp.zeros((s, HEAD_PAD - QK_DIM - MLA_ROPE), F32)
    cos_t = jnp.concatenate([jnp.ones((s, QK_DIM), F32), cos, cos, tail], axis=1)
    sin_t = jnp.concatenate([jnp.zeros((s, QK_DIM), F32), -sin, sin, tail], axis=1)
    return cos_t, sin_t


def kernel(x, ln1_g, w_in, fgate_b, lam_q1, lam_k1, lam_q2, lam_k2, diff_norm_g, q_norm_g, w_uq,
           kv_norm_g, w_ukv, w_o, ln2_g, w_up, conv_w, conv_b, w_down, final_g):
    s = x.shape[1]
    cos_t, sin_t = _rope_tables(s)
    n_fox = HEADS * QK_DIM
    n_diff = HEADS * HEAD_PAD
    for i in range(DEPTH):
        lam_init = 0.8 - 0.6 * math.exp(-0.3 * i)
        fb = jnp.pad(fgate_b[i], (0, 128 - HEADS)).reshape(1, 128)
        heads = _project(x, ln1_g[i].reshape(1, -1), _layout_w_in(w_in[i]), fb,
                         q_norm_g[i].reshape(1, -1), _layout_w_uq(w_uq[i]),
                         kv_norm_g[i].reshape(1, -1), _layout_w_ukv(w_ukv[i]), cos_t, sin_t)
        fq, fk, fvt, dq1, dq2, dk1, dk2, dvt, mq, mk, mvt = heads
        o_fox = _single_map_attention(fq, fk, fvt, chunked=False, name="fox_attn")
        lams = jnp.stack([lam_q1[i], lam_k1[i], lam_q2[i], lam_k2[i]])
        o_diff = _diff_attention(dq1, dq2, dk1, dk2, dvt, lams, diff_norm_g[i].reshape(1, -1),
                                 lam_init)
        o_mla = _single_map_attention(mq, mk, mvt, chunked=True, name="mla_attn")
        wo = w_o[i].astype(BF16)
        x = _ffn(x, o_fox, o_diff, o_mla,
                 _pad_head_rows(wo[:n_fox]), wo[n_fox:n_fox + n_diff],
                 _pad_head_rows(wo[n_fox + n_diff:]),
                 ln2_g[i].reshape(1, -1), w_up[i].astype(BF16), conv_w[i],
                 conv_b[i].reshape(1, -1), w_down[i].astype(BF16), final_g.reshape(1, -1),
                 final=(i == DEPTH - 1))
    return x
```

```python
import functools
import math

import jax
import jax.numpy as jnp
from jax import lax
from jax.experimental import pallas as pl
from jax.experimental.pallas import tpu as pltpu

F32 = jnp.float32
BF16 = jnp.bfloat16

D_MODEL = 1024
DEPTH = 2
CHUNK = 64
EPS = 1e-6
MASKED = -1e30

HEADS = 4
HEAD_PAD = 128
QK_DIM = 64
MLA_ROPE = 32
MLA_Q_RANK = 256
MLA_KV_RANK = 128
ROPE_THETA = 10000.0
D_FF = 2816
CONV_W = 3
BIAS_LANE = QK_DIM
ONES_LANE = QK_DIM

LOG2E = 1.4426950408889634

_GROUP = HEADS * HEAD_PAD
OFF_FQ, OFF_FK, OFF_FV = 0, _GROUP, 2 * _GROUP
OFF_FF = 3 * _GROUP
OFF_DQ1 = OFF_FF + 128
OFF_DQ2 = OFF_DQ1 + _GROUP
OFF_DK1 = OFF_DQ2 + _GROUP
OFF_DK2 = OFF_DK1 + _GROUP
OFF_DV = OFF_DK2 + _GROUP
OFF_CQ = OFF_DV + _GROUP
OFF_CKV = OFF_CQ + MLA_Q_RANK
OFF_KR = OFF_CKV + MLA_KV_RANK
OFF_KRP = OFF_KR + 128
PROJ_COLS = OFF_KRP + 128

TILE_K = 512
TILE_Q = 2 * TILE_K
TILE_PROJ = TILE_K
V_ROWS_ONES = 80
TILE_FFN = 512
FF_CHUNK = 256
HALO = 16
VMEM_LIMIT = 56 * 1024 * 1024


def _dot(a, b):
    return jnp.dot(a, b, preferred_element_type=F32)


def _dot_nt(a, b):
    return lax.dot_general(a, b, (((1,), (1,)), ((), ())), preferred_element_type=F32)


def _rms(x, g):
    return x * lax.rsqrt(jnp.mean(x * x, axis=-1, keepdims=True) + EPS) * g


def _split3(x):
    hi = x.astype(BF16)
    r1 = x - hi.astype(F32)
    mid = r1.astype(BF16)
    lo = (r1 - mid.astype(F32)).astype(BF16)
    return hi, mid, lo


def _const_spec(shape):
    nd = len(shape)
    return pl.BlockSpec(shape, lambda *_: (0,) * nd, pipeline_mode=pl.Buffered(1))


def _proj_kernel(x_ref, g1_ref, wa_ref, fb_ref, qng_ref, wuq_ref, kvng_ref, wukv_ref,
                 cos_ref, sin_ref,
                 fq_ref, fk_ref, fvt_ref, dq1_ref, dq2_ref, dk1_ref, dk2_ref, dvt_ref,
                 mq_ref, mk_ref, mvt_ref, carry_ref, *, tm):
    i = pl.program_id(1)

    @pl.when(i == 0)
    def _():
        carry_ref[...] = jnp.zeros_like(carry_ref)

    xn = _rms(x_ref[0], g1_ref[...]).astype(BF16)

    def proj(off, width):
        return _dot(xn, wa_ref[:, off:off + width])

    def head(t, h):
        return t[:, h * HEAD_PAD:(h + 1) * HEAD_PAD]

    lane = lax.broadcasted_iota(jnp.int32, (tm, HEAD_PAD), 1)
    ones3 = ((lane >= BIAS_LANE) & (lane < BIAS_LANE + 3)).astype(F32)
    one1 = (lane == ONES_LANE).astype(F32)
    qk_scale = QK_DIM ** -0.5 * LOG2E

    z = proj(OFF_FF, 128) + fb_ref[...]
    logf = jnp.minimum(z, 0.0) - jnp.log1p(jnp.exp(-jnp.abs(z)))
    rr = lax.broadcasted_iota(jnp.int32, (tm, tm), 0)
    cc = lax.broadcasted_iota(jnp.int32, (tm, tm), 1)
    tri = (cc <= rr).astype(BF16)
    l_hi, l_mid, l_lo = _split3(logf)
    cum = _dot(tri, l_hi) + _dot(tri, l_mid) + _dot(tri, l_lo) + carry_ref[...]
    carry_ref[...] = cum[tm - 1:tm, :]
    b_hi, b_mid, b_lo = _split3(cum * (-LOG2E))
    pr = lax.broadcasted_iota(jnp.int32, (128, _GROUP), 0)
    pc = lax.broadcasted_iota(jnp.int32, (128, _GROUP), 1)
    place = pc - pr * HEAD_PAD - BIAS_LANE
    fk_bias = (_dot(b_hi, (place == 0).astype(BF16)) + _dot(b_mid, (place == 1).astype(BF16))
               + _dot(b_lo, (place == 2).astype(BF16)))
    hq = proj(OFF_FQ, _GROUP)
    hk = proj(OFF_FK, _GROUP) + fk_bias
    hv = proj(OFF_FV, _GROUP)
    for h in range(HEADS):
        fq_ref[0, h] = (head(hq, h) * qk_scale + ones3).astype(BF16)
        fk_ref[0, h] = head(hk, h).astype(BF16)
        fvt_ref[0, h, 0] = (head(hv, h) + one1).T[:V_ROWS_ONES].astype(BF16)

    pos = (i * tm + lax.broadcasted_iota(jnp.int32, (tm, HEAD_PAD), 0)).astype(F32)
    hq1, hq2 = proj(OFF_DQ1, _GROUP), proj(OFF_DQ2, _GROUP)
    hk1, hk2 = proj(OFF_DK1, _GROUP), proj(OFF_DK2, _GROUP)
    hv = proj(OFF_DV, _GROUP)
    for h in range(HEADS):
        slope = 2.0 ** (-8.0 * (h + 1) / HEADS)
        a_hi, a_mid, a_lo = _split3(pos * (slope * LOG2E))
        kbias = jnp.where(lane == BIAS_LANE, a_hi.astype(F32),
                          jnp.where(lane == BIAS_LANE + 1, a_mid.astype(F32),
                                    jnp.where(lane == BIAS_LANE + 2, a_lo.astype(F32), 0.0)))
        dq1_ref[0, h] = (head(hq1, h) * qk_scale + ones3).astype(BF16)
        dq2_ref[0, h] = (head(hq2, h) * qk_scale + ones3).astype(BF16)
        dk1_ref[0, h] = (head(hk1, h) + kbias).astype(BF16)
        dk2_ref[0, h] = (head(hk2, h) + kbias).astype(BF16)
        dvt_ref[0, h, 0] = head(hv, h).T.astype(BF16)

    cqn = _rms(proj(OFF_CQ, MLA_Q_RANK), qng_ref[...]).astype(BF16)
    ckvn = _rms(proj(OFF_CKV, MLA_KV_RANK), kvng_ref[...]).astype(BF16)
    cos = cos_ref[...]
    sin = sin_ref[...]
    krot = proj(OFF_KR, 128) * cos + proj(OFF_KRP, 128) * sin
    qa = _dot(cqn, wuq_ref[:, :_GROUP])
    qb = _dot(cqn, wuq_ref[:, _GROUP:])
    kk = _dot(ckvn, wukv_ref[:, :_GROUP])
    vv = _dot(ckvn, wukv_ref[:, _GROUP:])
    mla_scale = (QK_DIM + MLA_ROPE) ** -0.5 * LOG2E
    for h in range(HEADS):
        mq_ref[0, h] = ((head(qa, h) * cos + head(qb, h) * sin) * mla_scale).astype(BF16)
        mk_ref[0, h] = (head(kk, h) + krot).astype(BF16)
        mvt_ref[0, h, 0] = (head(vv, h) + one1).T[:V_ROWS_ONES].astype(BF16)


def _project(x, g1, wa, fb, qng, wuq, kvng, wukv, cos_t, sin_t):
    b, s, _ = x.shape
    tm = TILE_PROJ
    rows = (jax.ShapeDtypeStruct((b, HEADS, s, HEAD_PAD), BF16),
            pl.BlockSpec((1, HEADS, tm, HEAD_PAD), lambda bi, i: (bi, 0, i, 0)))
    def cols(v_rows):
        return (jax.ShapeDtypeStruct((b, HEADS, s // tm, v_rows, tm), BF16),
                pl.BlockSpec((1, HEADS, 1, v_rows, tm), lambda bi, i: (bi, 0, i, 0, 0)))

    outs = [rows, rows, cols(V_ROWS_ONES), rows, rows, rows, rows, cols(HEAD_PAD),
            rows, rows, cols(V_ROWS_ONES)]
    row_tab = pl.BlockSpec((tm, HEAD_PAD), lambda bi, i: (i, 0))
    return pl.pallas_call(
        functools.partial(_proj_kernel, tm=tm),
        grid=(b, s // tm),
        in_specs=[
            pl.BlockSpec((1, tm, D_MODEL), lambda bi, i: (bi, i, 0)),
            _const_spec((1, D_MODEL)),
            _const_spec((D_MODEL, PROJ_COLS)),
            _const_spec((1, 128)),
            _const_spec((1, MLA_Q_RANK)),
            _const_spec((MLA_Q_RANK, 2 * _GROUP)),
            _const_spec((1, MLA_KV_RANK)),
            _const_spec((MLA_KV_RANK, 2 * _GROUP)),
            row_tab, row_tab,
        ],
        out_specs=[o[1] for o in outs],
        out_shape=[o[0] for o in outs],
        scratch_shapes=[pltpu.VMEM((1, 128), F32)],
        compiler_params=pltpu.CompilerParams(
            dimension_semantics=("arbitrary", "arbitrary"), vmem_limit_bytes=VMEM_LIMIT),
        name="proj",
    )(x, g1, wa, fb, qng, wuq, kvng, wukv, cos_t, sin_t)


def _tile_ids(t):
    key = lax.broadcasted_iota(jnp.int32, (t, t), 0)
    qry = lax.broadcasted_iota(jnp.int32, (t, t), 1)
    return key, qry


def _online_step(s, vt, m_ref, acc_ref, l_ref=None, lanes=slice(None)):
    m_prev = m_ref[:, lanes]
    m_new = jnp.maximum(m_prev, jnp.max(s, axis=0, keepdims=True))
    p = jnp.exp2(s - m_new)
    alpha = jnp.exp2(m_prev - m_new)
    if l_ref is not None:
        l_ref[:, lanes] = alpha * l_ref[:, lanes] + jnp.sum(p, axis=0, keepdims=True)
    acc_ref[:, lanes] = alpha * acc_ref[:, lanes] + _dot(vt, p.astype(BF16))
    m_ref[:, lanes] = m_new


def _run_tiles(i, compute_scores, consume, compute_half, consume_diag0, consume_diag1,
               buf_a, buf_b):
    compute_scores(0, buf_a)

    def body(jj, c):
        j = 2 * jj
        compute_scores(j + 1, buf_b)
        consume(j, buf_a)
        compute_scores(j + 2, buf_a)
        consume(j + 1, buf_b)
        return c

    lax.fori_loop(0, i, body, 0)
    compute_half(2 * i + 1, buf_b)
    consume_diag0(2 * i, buf_a)
    consume_diag1(2 * i + 1, buf_b)


def _diag_bias(tk, chunked, slope2=None):
    key, qry = _tile_ids(tk)
    keep = (key // CHUNK) <= (qry // CHUNK) if chunked else key <= qry
    if slope2 is None:
        return jnp.where(keep, 0.0, MASKED)
    return jnp.where(keep, jnp.maximum(key - qry, 0).astype(F32) * -slope2, MASKED)


def _single_map_kernel(q_ref, k_ref, vt_ref, o_ref, sa_ref, sb_ref, bias_ref, m_ref, acc_ref,
                       *, tk, chunked):
    i = pl.program_id(2)

    @pl.when(i == 0)
    def _():
        bias_ref[...] = _diag_bias(tk, chunked)

    m_ref[...] = jnp.full_like(m_ref, MASKED)
    acc_ref[...] = jnp.zeros_like(acc_ref)

    def k_tile(kt):
        return k_ref[0, 0, pl.ds(pl.multiple_of(kt * tk, tk), tk), :]

    def compute_scores(kt, buf):
        buf[0][...] = _dot_nt(k_tile(kt), q_ref[0, 0])

    def consume(kt, buf):
        _online_step(buf[0][...], vt_ref[0, 0, kt], m_ref, acc_ref)

    def compute_half(kt, buf):
        buf[0][:, :tk] = _dot_nt(k_tile(kt), q_ref[0, 0, tk:, :])

    def consume_diag0(kt, buf):
        s = jnp.concatenate([buf[0][:, :tk] + bias_ref[...], buf[0][:, tk:]], axis=1)
        _online_step(s, vt_ref[0, 0, kt], m_ref, acc_ref)

    def consume_diag1(kt, buf):
        _online_step(buf[0][:, :tk] + bias_ref[...], vt_ref[0, 0, kt], m_ref, acc_ref,
                     lanes=slice(tk, None))

    _run_tiles(i, compute_scores, consume, compute_half, consume_diag0, consume_diag1,
               (sa_ref,), (sb_ref,))

    acc = acc_ref[...]
    o_t = acc[:QK_DIM] * (1.0 / acc[ONES_LANE:ONES_LANE + 1, :])
    o_t = jnp.concatenate([o_t, jnp.zeros((HEAD_PAD - QK_DIM, 2 * tk), F32)], axis=0)
    o_ref[0] = o_t.T.astype(BF16)


def _diff_kernel(q1_ref, q2_ref, k1_ref, k2_ref, vt_ref, lam_ref, g_ref, o_ref,
                 s1a_ref, s2a_ref, s1b_ref, s2b_ref, bias_ref,
                 m1_ref, l1_ref, acc1_ref, m2_ref, l2_ref, acc2_ref, *, tk, lam_init):
    h = pl.program_id(1)
    i = pl.program_id(2)

    @pl.when(i == 0)
    def _():
        slope2 = jnp.exp2(jnp.full((1, 1), -2.0, F32) * (h + 1).astype(F32)) * (2.0 * LOG2E)
        bias_ref[...] = _diag_bias(tk, True, slope2)

    for m_ref, l_ref, acc_ref in ((m1_ref, l1_ref, acc1_ref), (m2_ref, l2_ref, acc2_ref)):
        m_ref[...] = jnp.full_like(m_ref, MASKED)
        l_ref[...] = jnp.zeros_like(l_ref)
        acc_ref[...] = jnp.zeros_like(acc_ref)

    maps = ((q1_ref, k1_ref, m1_ref, acc1_ref, l1_ref), (q2_ref, k2_ref, m2_ref, acc2_ref, l2_ref))

    def k_tile(k_ref, kt):
        return k_ref[0, 0, pl.ds(pl.multiple_of(kt * tk, tk), tk), :]

    def compute_scores(kt, buf):
        for (q_ref, k_ref, _, _, _), s_ref in zip(maps, buf):
            s_ref[...] = _dot_nt(k_tile(k_ref, kt), q_ref[0, 0])

    def consume(kt, buf):
        vt = vt_ref[0, 0, kt]
        for (_, _, m_ref, acc_ref, l_ref), s_ref in zip(maps, buf):
            _online_step(s_ref[...], vt, m_ref, acc_ref, l_ref)

    def compute_half(kt, buf):
        for (q_ref, k_ref, _, _, _), s_ref in zip(maps, buf):
            s_ref[:, :tk] = _dot_nt(k_tile(k_ref, kt), q_ref[0, 0, tk:, :])

    def consume_diag0(kt, buf):
        vt = vt_ref[0, 0, kt]
        for (_, _, m_ref, acc_ref, l_ref), s_ref in zip(maps, buf):
            s = jnp.concatenate([s_ref[:, :tk] + bias_ref[...], s_ref[:, tk:]], axis=1)
            _online_step(s, vt, m_ref, acc_ref, l_ref)

    def consume_diag1(kt, buf):
        vt = vt_ref[0, 0, kt]
        for (_, _, m_ref, acc_ref, l_ref), s_ref in zip(maps, buf):
            _online_step(s_ref[:, :tk] + bias_ref[...], vt, m_ref, acc_ref, l_ref,
                         lanes=slice(tk, None))

    _run_tiles(i, compute_scores, consume, compute_half, consume_diag0, consume_diag1,
               (s1a_ref, s2a_ref), (s1b_ref, s2b_ref))

    lams = lam_ref[...]
    lam = (jnp.exp(jnp.sum(lams[0:1] * lams[1:2], axis=-1, keepdims=True))
           - jnp.exp(jnp.sum(lams[2:3] * lams[3:4], axis=-1, keepdims=True)) + lam_init)
    o_t = acc1_ref[...] * (1.0 / l1_ref[...]) - lam * (acc2_ref[...] * (1.0 / l2_ref[...]))
    o_ref[0] = (_rms(o_t.T, g_ref[...]) * (1.0 - lam_init)).astype(BF16)


def _attn_specs(s, v_rows):
    tq, tk = TILE_Q, TILE_K
    q_spec = pl.BlockSpec((1, 1, tq, HEAD_PAD), lambda b, h, i: (b, h, i, 0))
    k_spec = pl.BlockSpec((1, 1, s, HEAD_PAD), lambda b, h, i: (b, h, 0, 0))
    vt_spec = pl.BlockSpec((1, 1, s // tk, v_rows, tk), lambda b, h, i: (b, h, 0, 0, 0))
    o_spec = pl.BlockSpec((1, tq, HEAD_PAD), lambda b, h, i: (b, i, h))
    return q_spec, k_spec, vt_spec, o_spec


def _single_map_attention(q, k, vt, *, chunked, name):
    b, _, s, _ = q.shape
    tq, tk = TILE_Q, TILE_K
    q_spec, k_spec, vt_spec, o_spec = _attn_specs(s, V_ROWS_ONES)
    score = pltpu.VMEM((tk, tq), F32)
    return pl.pallas_call(
        functools.partial(_single_map_kernel, tk=tk, chunked=chunked),
        grid=(b, HEADS, s // tq),
        in_specs=[q_spec, k_spec, vt_spec],
        out_specs=o_spec,
        out_shape=jax.ShapeDtypeStruct((b, s, _GROUP), BF16),
        scratch_shapes=[score, score, pltpu.VMEM((tk, tk), F32),
                        pltpu.VMEM((1, tq), F32), pltpu.VMEM((V_ROWS_ONES, tq), F32)],
        compiler_params=pltpu.CompilerParams(
            dimension_semantics=("arbitrary",) * 3, vmem_limit_bytes=VMEM_LIMIT),
        name=name,
    )(q, k, vt)


def _diff_attention(q1, q2, k1, k2, vt, lams, g, lam_init):
    b, _, s, _ = q1.shape
    tq, tk = TILE_Q, TILE_K
    q_spec, k_spec, vt_spec, o_spec = _attn_specs(s, HEAD_PAD)
    score = pltpu.VMEM((tk, tq), F32)
    stat = pltpu.VMEM((1, tq), F32)
    acc = pltpu.VMEM((HEAD_PAD, tq), F32)
    return pl.pallas_call(
        functools.partial(_diff_kernel, tk=tk, lam_init=lam_init),
        grid=(b, HEADS, s // tq),
        in_specs=[q_spec, q_spec, k_spec, k_spec, vt_spec,
                  pl.BlockSpec((4, QK_DIM), lambda b, h, i: (0, 0)),
                  pl.BlockSpec((1, HEAD_PAD), lambda b, h, i: (0, 0))],
        out_specs=o_spec,
        out_shape=jax.ShapeDtypeStruct((b, s, _GROUP), BF16),
        scratch_shapes=[score, score, score, score, pltpu.VMEM((tk, tk), F32),
                        stat, stat, acc, stat, stat, acc],
        compiler_params=pltpu.CompilerParams(
            dimension_semantics=("arbitrary",) * 3, vmem_limit_bytes=VMEM_LIMIT),
        name="diff_attn",
    )(q1, q2, k1, k2, vt, lams, g)


def _ffn_kernel(x_ref, of_ref, od_ref, om_ref, wof_ref, wod_ref, wom_ref, g2_ref,
                wup_ref, cw_ref, cb_ref, wdn_ref, gf_ref, o_ref, xn_ref, act_ref,
                *, tm, final):
    i = pl.program_id(1)

    @pl.when(i == 0)
    def _():
        xn_ref[0:HALO, :] = jnp.zeros((HALO, D_MODEL), BF16)

    @pl.when(i > 0)
    def _():
        xn_ref[0:HALO, :] = xn_ref[tm:tm + HALO, :]

    x1 = (x_ref[0] + _dot(of_ref[0], wof_ref[...]) + _dot(od_ref[0], wod_ref[...])
          + _dot(om_ref[0], wom_ref[...]))
    xn_ref[HALO:HALO + tm, :] = _rms(x1, g2_ref[...]).astype(BF16)

    def conv(up, off):
        cw = cw_ref[:, off:off + FF_CHUNK]
        y = cb_ref[:, off:off + FF_CHUNK]
        for j in range(CONV_W):
            lo = HALO - (CONV_W - 1) + j
            y = y + up[lo:lo + tm, :] * cw[j:j + 1, :]
        return y

    xn = xn_ref[...]
    for c in range(D_FF // FF_CHUNK):
        og = c * FF_CHUNK
        ou = D_FF + og
        yg = conv(_dot(xn, wup_ref[:, og:og + FF_CHUNK]), og)
        yu = conv(_dot(xn, wup_ref[:, ou:ou + FF_CHUNK]), ou)
        act = yg * (1.0 / (1.0 + jnp.exp(-yg))) * yu
        act_ref[:, og:og + FF_CHUNK] = act.astype(BF16)

    x2 = x1 + _dot(act_ref[...], wdn_ref[...])
    if final:
        x2 = _rms(x2, gf_ref[...])
    o_ref[0] = x2


def _ffn(x, o_fox, o_diff, o_mla, wo_f, wo_d, wo_m, g2, w_up, conv_w, conv_b, w_down, g_final,
         final):
    b, s, _ = x.shape
    tm = TILE_FFN
    row = lambda width: pl.BlockSpec((1, tm, width), lambda bi, i: (bi, i, 0))
    return pl.pallas_call(
        functools.partial(_ffn_kernel, tm=tm, final=final),
        grid=(b, s // tm),
        in_specs=[
            row(D_MODEL), row(_GROUP), row(_GROUP), row(_GROUP),
            _const_spec((_GROUP, D_MODEL)), _const_spec((_GROUP, D_MODEL)),
            _const_spec((_GROUP, D_MODEL)),
            _const_spec((1, D_MODEL)),
            _const_spec((D_MODEL, 2 * D_FF)),
            _const_spec((CONV_W, 2 * D_FF)),
            _const_spec((1, 2 * D_FF)),
            _const_spec((D_FF, D_MODEL)),
            _const_spec((1, D_MODEL)),
        ],
        out_specs=row(D_MODEL),
        out_shape=jax.ShapeDtypeStruct((b, s, D_MODEL), F32),
        scratch_shapes=[pltpu.VMEM((HALO + tm, D_MODEL), BF16), pltpu.VMEM((tm, D_FF), BF16)],
        compiler_params=pltpu.CompilerParams(
            dimension_semantics=("arbitrary", "arbitrary"), vmem_limit_bytes=VMEM_LIMIT),
        name="ffn",
    )(x, o_fox, o_diff, o_mla, wo_f, wo_d, wo_m, g2, w_up, conv_w, conv_b, w_down, g_final)


def _pad_heads(w, width):
    rows = w.shape[0]
    w = w.reshape(rows, HEADS, width)
    w = jnp.pad(w, ((0, 0), (0, 0), (0, HEAD_PAD - width)))
    return w.reshape(rows, HEADS * HEAD_PAD)


def _rot_half_cols(w):
    half = MLA_ROPE // 2
    return jnp.concatenate([w[..., half:], w[..., :half]], axis=-1)


def _at_rope_lanes(w):
    return jnp.pad(w, ((0, 0), (QK_DIM, HEAD_PAD - QK_DIM - MLA_ROPE)))


def _layout_w_in(w):
    sizes = (256, 256, 256, 4, 512, 512, 512, MLA_Q_RANK, MLA_KV_RANK, MLA_ROPE)
    splits = [sum(sizes[:n]) for n in range(1, len(sizes))]
    fq, fk, fv, ff, dq, dk, dv, cq, ckv, kr = jnp.split(w, splits, axis=1)
    dq = dq.reshape(D_MODEL, HEADS, 2, QK_DIM)
    dk = dk.reshape(D_MODEL, HEADS, 2, QK_DIM)
    cols = [
        _pad_heads(fq, QK_DIM), _pad_heads(fk, QK_DIM), _pad_heads(fv, QK_DIM),
        jnp.pad(ff, ((0, 0), (0, 128 - HEADS))),
        _pad_heads(dq[:, :, 0].reshape(D_MODEL, -1), QK_DIM),
        _pad_heads(dq[:, :, 1].reshape(D_MODEL, -1), QK_DIM),
        _pad_heads(dk[:, :, 0].reshape(D_MODEL, -1), QK_DIM),
        _pad_heads(dk[:, :, 1].reshape(D_MODEL, -1), QK_DIM),
        dv, cq, ckv, _at_rope_lanes(kr), _at_rope_lanes(_rot_half_cols(kr)),
    ]
    out = jnp.concatenate(cols, axis=1).astype(BF16)
    assert out.shape == (D_MODEL, PROJ_COLS)
    return out


def _layout_w_uq(w):
    w = w.reshape(MLA_Q_RANK, HEADS, QK_DIM + MLA_ROPE)
    nope, rope = w[..., :QK_DIM], w[..., QK_DIM:]
    pad = jnp.zeros((MLA_Q_RANK, HEADS, HEAD_PAD - QK_DIM - MLA_ROPE), w.dtype)
    direct = jnp.concatenate([nope, rope, pad], axis=-1)
    swapped = jnp.concatenate([jnp.zeros_like(nope), _rot_half_cols(rope), pad], axis=-1)
    return jnp.concatenate([direct.reshape(MLA_Q_RANK, -1), swapped.reshape(MLA_Q_RANK, -1)],
                           axis=1).astype(BF16)


def _layout_w_ukv(w):
    w = w.reshape(MLA_KV_RANK, HEADS, 2 * QK_DIM)
    k = _pad_heads(w[..., :QK_DIM].reshape(MLA_KV_RANK, -1), QK_DIM)
    v = _pad_heads(w[..., QK_DIM:].reshape(MLA_KV_RANK, -1), QK_DIM)
    return jnp.concatenate([k, v], axis=1).astype(BF16)


def _pad_head_rows(w):
    w = w.reshape(HEADS, QK_DIM, -1)
    w = jnp.pad(w, ((0, 0), (0, HEAD_PAD - QK_DIM), (0, 0)))
    return w.reshape(HEADS * HEAD_PAD, -1)


def _rope_tables(s):
    half = MLA_ROPE // 2
    pos = jnp.arange(s, dtype=F32)
    inv = 1.0 / (ROPE_THETA ** (jnp.arange(half, dtype=F32) / half))
    ang = pos[:, None] * inv[None, :]
    cos, sin = jnp.cos(ang), jnp.sin(ang)
    tail = jnp.zeros((s, HEAD_PAD - QK_DIM - MLA_ROPE), F32)
    cos_t = jnp.concatenate([jnp.ones((s, QK_DIM), F32), cos, cos, tail], axis=1)
    sin_t = jnp.concatenate([jnp.zeros((s, QK_DIM), F32), -sin, sin, tail], axis=1)
    return cos_t, sin_t


def kernel(x, ln1_g, w_in, fgate_b, lam_q1, lam_k1, lam_q2, lam_k2, diff_norm_g, q_norm_g, w_uq,
           kv_norm_g, w_ukv, w_o, ln2_g, w_up, conv_w, conv_b, w_down, final_g):
    s = x.shape[1]
    cos_t, sin_t = _rope_tables(s)
    n_fox = HEADS * QK_DIM
    n_diff = HEADS * HEAD_PAD
    for i in range(DEPTH):
        lam_init = 0.8 - 0.6 * math.exp(-0.3 * i)
        fb = jnp.pad(fgate_b[i], (0, 128 - HEADS)).reshape(1, 128)
        heads = _project(x, ln1_g[i].reshape(1, -1), _layout_w_in(w_in[i]), fb,
                         q_norm_g[i].reshape(1, -1), _layout_w_uq(w_uq[i]),
                         kv_norm_g[i].reshape(1, -1), _layout_w_ukv(w_ukv[i]), cos_t, sin_t)
        fq, fk, fvt, dq1, dq2, dk1, dk2, dvt, mq, mk, mvt = heads
        o_fox = _single_map_attention(fq, fk, fvt, chunked=False, name="fox_attn")
        lams = jnp.stack([lam_q1[i], lam_k1[i], lam_q2[i], lam_k2[i]])
        o_diff = _diff_attention(dq1, dq2, dk1, dk2, dvt, lams, diff_norm_g[i].reshape(1, -1),
                                 lam_init)
        o_mla = _single_map_attention(mq, mk, mvt, chunked=True, name="mla_attn")
        wo = w_o[i].astype(BF16)
        x = _ffn(x, o_fox, o_diff, o_mla,
                 _pad_head_rows(wo[:n_fox]), wo[n_fox:n_fox + n_diff],
                 _pad_head_rows(wo[n_fox + n_diff:]),
                 ln2_g[i].reshape(1, -1), w_up[i].astype(BF16), conv_w[i],
                 conv_b[i].reshape(1, -1), w_down[i].astype(BF16), final_g.reshape(1, -1),
                 final=(i == DEPTH - 1))
    return x
```

```python
import functools
import math

import jax
import jax.numpy as jnp
from jax import lax
from jax.experimental import pallas as pl
from jax.experimental.pallas import tpu as pltpu

F32 = jnp.float32
BF16 = jnp.bfloat16

D_MODEL = 1024
DEPTH = 2
CHUNK = 64
EPS = 1e-6
MASKED = -1e30

HEADS = 4
HEAD_PAD = 128
QK_DIM = 64
MLA_ROPE = 32
MLA_Q_RANK = 256
MLA_KV_RANK = 128
ROPE_THETA = 10000.0
D_FF = 2816
CONV_W = 3
BIAS_LANE = QK_DIM
ONES_LANE = QK_DIM

LOG2E = 1.4426950408889634

_GROUP = HEADS * HEAD_PAD
_FOX = HEADS * QK_DIM
OFF_FQ, OFF_FK, OFF_FV = 0, _FOX, 2 * _FOX
OFF_FF = 3 * _FOX
OFF_DQ = OFF_FF + 128
OFF_DK = OFF_DQ + _GROUP
OFF_DV = OFF_DK + _GROUP
OFF_CQ = OFF_DV + _GROUP
OFF_CKV = OFF_CQ + MLA_Q_RANK
OFF_KR = OFF_CKV + MLA_KV_RANK
OFF_KRP = OFF_KR + 128
PROJ_COLS = OFF_KRP + 128

TILE_K = 512
TILE_Q = 2 * TILE_K
TILE_PROJ = TILE_K
V_ROWS_ONES = 80
TILE_FFN = 512
FF_CHUNK = 256
HALO = 16
VMEM_LIMIT = 56 * 1024 * 1024


def _dot(a, b):
    return jnp.dot(a, b, preferred_element_type=F32)


def _dot_nt(a, b):
    return lax.dot_general(a, b, (((1,), (1,)), ((), ())), preferred_element_type=F32)


def _rms(x, g):
    return x * lax.rsqrt(jnp.mean(x * x, axis=-1, keepdims=True) + EPS) * g


def _split3(x):
    hi = x.astype(BF16)
    r1 = x - hi.astype(F32)
    mid = r1.astype(BF16)
    lo = (r1 - mid.astype(F32)).astype(BF16)
    return hi, mid, lo


def _const_spec(shape):
    nd = len(shape)
    return pl.BlockSpec(shape, lambda *_: (0,) * nd, pipeline_mode=pl.Buffered(1))


def _proj_kernel(x_ref, g1_ref, wa_ref, fb_ref, qng_ref, wuq_ref, kvng_ref, wukv_ref,
                 cos_ref, sin_ref,
                 fq_ref, fk_ref, fvt_ref, dq1_ref, dq2_ref, dk1_ref, dk2_ref, dvt_ref,
                 mq_ref, mk_ref, mvt_ref, carry_ref, *, tm):
    i = pl.program_id(1)

    @pl.when(i == 0)
    def _():
        carry_ref[...] = jnp.zeros_like(carry_ref)

    xn = _rms(x_ref[0], g1_ref[...]).astype(BF16)

    def proj(off, width):
        return _dot(xn, wa_ref[:, off:off + width])

    def head(t, h):
        return t[:, h * HEAD_PAD:(h + 1) * HEAD_PAD]

    lane = lax.broadcasted_iota(jnp.int32, (tm, HEAD_PAD), 1)
    low = lane < QK_DIM
    extra = jnp.where(low, lane, lane - QK_DIM)
    in_half = (low, jnp.logical_not(low))
    ones3 = tuple(jnp.where(jnp.logical_not(m) & (extra < 3), 1.0, 0.0) for m in in_half)
    one1 = tuple(jnp.where(jnp.logical_not(m) & (extra == 0), 1.0, 0.0) for m in in_half)
    qk_scale = QK_DIM ** -0.5 * LOG2E

    def v_rows(t, half):
        t = t.T
        if half == 0:
            return t[:V_ROWS_ONES]
        return jnp.concatenate([t[QK_DIM:], t[:V_ROWS_ONES - QK_DIM]], axis=0)

    z = proj(OFF_FF, 128) + fb_ref[...]
    logf = jnp.minimum(z, 0.0) - jnp.log1p(jnp.exp(-jnp.abs(z)))
    rr = lax.broadcasted_iota(jnp.int32, (tm, tm), 0)
    cc = lax.broadcasted_iota(jnp.int32, (tm, tm), 1)
    tri = (cc <= rr).astype(BF16)
    l_hi, l_mid, l_lo = _split3(logf)
    cum = _dot(tri, l_hi) + _dot(tri, l_mid) + _dot(tri, l_lo) + carry_ref[...]
    carry_ref[...] = cum[tm - 1:tm, :]
    b_hi, b_mid, b_lo = _split3(cum * (-LOG2E))
    pr = lax.broadcasted_iota(jnp.int32, (128, _GROUP), 0)
    pc = lax.broadcasted_iota(jnp.int32, (128, _GROUP), 1)
    place = pc - pr * HEAD_PAD - jnp.where(pr % 2 == 0, QK_DIM, 0)
    fk_bias = (_dot(b_hi, (place == 0).astype(BF16)) + _dot(b_mid, (place == 1).astype(BF16))
               + _dot(b_lo, (place == 2).astype(BF16)))
    hq = proj(OFF_FQ, _FOX)
    hk = proj(OFF_FK, _FOX)
    hv = proj(OFF_FV, _FOX)
    for h in range(HEADS):
        pair, half = divmod(h, 2)
        keep = in_half[half]
        fq_ref[0, h] = jnp.where(keep, head(hq, pair) * qk_scale, ones3[half]).astype(BF16)
        fk_ref[0, h] = jnp.where(keep, head(hk, pair), head(fk_bias, h)).astype(BF16)
        fvt_ref[0, h, 0] = v_rows(jnp.where(keep, head(hv, pair), one1[half]), half).astype(BF16)

    pos = (i * tm + lax.broadcasted_iota(jnp.int32, (tm, HEAD_PAD), 0)).astype(F32)
    hq = proj(OFF_DQ, _GROUP)
    hk = proj(OFF_DK, _GROUP)
    hv = proj(OFF_DV, _GROUP)
    for h in range(HEADS):
        slope = 2.0 ** (-8.0 * (h + 1) / HEADS)
        a_hi, a_mid, a_lo = _split3(pos * (slope * LOG2E))
        kbias = jnp.where(extra == 0, a_hi.astype(F32),
                          jnp.where(extra == 1, a_mid.astype(F32),
                                    jnp.where(extra == 2, a_lo.astype(F32), 0.0)))
        q = head(hq, h) * qk_scale
        k = head(hk, h)
        for half, (q_ref, k_ref) in enumerate(((dq1_ref, dk1_ref), (dq2_ref, dk2_ref))):
            q_ref[0, h] = jnp.where(in_half[half], q, ones3[half]).astype(BF16)
            k_ref[0, h] = jnp.where(in_half[half], k, kbias).astype(BF16)
        dvt_ref[0, h, 0] = head(hv, h).T.astype(BF16)

    cqn = _rms(proj(OFF_CQ, MLA_Q_RANK), qng_ref[...]).astype(BF16)
    ckvn = _rms(proj(OFF_CKV, MLA_KV_RANK), kvng_ref[...]).astype(BF16)
    cos = cos_ref[...]
    sin = sin_ref[...]
    krot = proj(OFF_KR, 128) * cos + proj(OFF_KRP, 128) * sin
    qa = _dot(cqn, wuq_ref[:, :_GROUP])
    qb = _dot(cqn, wuq_ref[:, _GROUP:])
    kk = _dot(ckvn, wukv_ref[:, :_GROUP])
    vv = _dot(ckvn, wukv_ref[:, _GROUP:])
    mla_scale = (QK_DIM + MLA_ROPE) ** -0.5 * LOG2E
    for h in range(HEADS):
        mq_ref[0, h] = ((head(qa, h) * cos + head(qb, h) * sin) * mla_scale).astype(BF16)
        mk_ref[0, h] = (head(kk, h) + krot).astype(BF16)
        mvt_ref[0, h, 0] = v_rows(head(vv, h) + one1[0], 0).astype(BF16)


def _project(x, g1, wa, fb, qng, wuq, kvng, wukv, cos_t, sin_t):
    b, s, _ = x.shape
    tm = TILE_PROJ
    rows = (jax.ShapeDtypeStruct((b, HEADS, s, HEAD_PAD), BF16),
            pl.BlockSpec((1, HEADS, tm, HEAD_PAD), lambda bi, i: (bi, 0, i, 0)))
    def cols(v_rows):
        return (jax.ShapeDtypeStruct((b, HEADS, s // tm, v_rows, tm), BF16),
                pl.BlockSpec((1, HEADS, 1, v_rows, tm), lambda bi, i: (bi, 0, i, 0, 0)))

    outs = [rows, rows, cols(V_ROWS_ONES), rows, rows, rows, rows, cols(HEAD_PAD),
            rows, rows, cols(V_ROWS_ONES)]
    row_tab = pl.BlockSpec((tm, HEAD_PAD), lambda bi, i: (i, 0))
    return pl.pallas_call(
        functools.partial(_proj_kernel, tm=tm),
        grid=(b, s // tm),
        in_specs=[
            pl.BlockSpec((1, tm, D_MODEL), lambda bi, i: (bi, i, 0)),
            _const_spec((1, D_MODEL)),
            _const_spec((D_MODEL, PROJ_COLS)),
            _const_spec((1, 128)),
            _const_spec((1, MLA_Q_RANK)),
            _const_spec((MLA_Q_RANK, 2 * _GROUP)),
            _const_spec((1, MLA_KV_RANK)),
            _const_spec((MLA_KV_RANK, 2 * _GROUP)),
            row_tab, row_tab,
        ],
        out_specs=[o[1] for o in outs],
        out_shape=[o[0] for o in outs],
        scratch_shapes=[pltpu.VMEM((1, 128), F32)],
        compiler_params=pltpu.CompilerParams(
            dimension_semantics=("arbitrary", "arbitrary"), vmem_limit_bytes=VMEM_LIMIT),
        name="proj",
    )(x, g1, wa, fb, qng, wuq, kvng, wukv, cos_t, sin_t)


def _tile_ids(t):
    key = lax.broadcasted_iota(jnp.int32, (t, t), 0)
    qry = lax.broadcasted_iota(jnp.int32, (t, t), 1)
    return key, qry


def _online_step(s, vt, m_ref, acc_ref, l_ref=None, lanes=slice(None)):
    m_prev = m_ref[:, lanes]
    m_new = jnp.maximum(m_prev, jnp.max(s, axis=0, keepdims=True))
    p = jnp.exp2(s - m_new)
    alpha = jnp.exp2(m_prev - m_new)
    if l_ref is not None:
        l_ref[:, lanes] = alpha * l_ref[:, lanes] + jnp.sum(p, axis=0, keepdims=True)
    acc_ref[:, lanes] = alpha * acc_ref[:, lanes] + _dot(vt, p.astype(BF16))
    m_ref[:, lanes] = m_new


def _run_tiles(i, compute_scores, consume, compute_half, consume_diag0, consume_diag1,
               buf_a, buf_b):
    compute_scores(0, buf_a)

    def body(jj, c):
        j = 2 * jj
        compute_scores(j + 1, buf_b)
        consume(j, buf_a)
        compute_scores(j + 2, buf_a)
        consume(j + 1, buf_b)
        return c

    lax.fori_loop(0, i, body, 0)
    compute_half(2 * i + 1, buf_b)
    consume_diag0(2 * i, buf_a)
    consume_diag1(2 * i + 1, buf_b)


def _diag_bias(tk, chunked, slope2=None):
    key, qry = _tile_ids(tk)
    keep = (key // CHUNK) <= (qry // CHUNK) if chunked else key <= qry
    if slope2 is None:
        return jnp.where(keep, 0.0, MASKED)
    return jnp.where(keep, jnp.maximum(key - qry, 0).astype(F32) * -slope2, MASKED)


def _single_map_kernel(q_ref, k_ref, vt_ref, o_ref, sa_ref, sb_ref, bias_ref, m_ref, acc_ref,
                       *, tk, chunked):
    i = pl.program_id(2)

    @pl.when(i == 0)
    def _():
        bias_ref[...] = _diag_bias(tk, chunked)

    m_ref[...] = jnp.full_like(m_ref, MASKED)
    acc_ref[...] = jnp.zeros_like(acc_ref)

    def k_tile(kt):
        return k_ref[0, 0, pl.ds(pl.multiple_of(kt * tk, tk), tk), :]

    def compute_scores(kt, buf):
        buf[0][...] = _dot_nt(k_tile(kt), q_ref[0, 0])

    def consume(kt, buf):
        _online_step(buf[0][...], vt_ref[0, 0, kt], m_ref, acc_ref)

    def compute_half(kt, buf):
        buf[0][:, :tk] = _dot_nt(k_tile(kt), q_ref[0, 0, tk:, :])

    def consume_diag0(kt, buf):
        s = jnp.concatenate([buf[0][:, :tk] + bias_ref[...], buf[0][:, tk:]], axis=1)
        _online_step(s, vt_ref[0, 0, kt], m_ref, acc_ref)

    def consume_diag1(kt, buf):
        _online_step(buf[0][:, :tk] + bias_ref[...], vt_ref[0, 0, kt], m_ref, acc_ref,
                     lanes=slice(tk, None))

    _run_tiles(i, compute_scores, consume, compute_half, consume_diag0, consume_diag1,
               (sa_ref,), (sb_ref,))

    acc = acc_ref[...]
    o_t = acc[:QK_DIM] * (1.0 / acc[ONES_LANE:ONES_LANE + 1, :])
    o_t = jnp.concatenate([o_t, jnp.zeros((HEAD_PAD - QK_DIM, 2 * tk), F32)], axis=0)
    o_ref[0] = o_t.T.astype(BF16)


def _diff_kernel(q1_ref, q2_ref, k1_ref, k2_ref, vt_ref, lam_ref, g_ref, o_ref,
                 s1a_ref, s2a_ref, s1b_ref, s2b_ref, bias_ref,
                 m1_ref, l1_ref, acc1_ref, m2_ref, l2_ref, acc2_ref, *, tk, lam_init):
    h = pl.program_id(1)
    i = pl.program_id(2)

    @pl.when(i == 0)
    def _():
        slope2 = jnp.exp2(jnp.full((1, 1), -2.0, F32) * (h + 1).astype(F32)) * (2.0 * LOG2E)
        bias_ref[...] = _diag_bias(tk, True, slope2)

    for m_ref, l_ref, acc_ref in ((m1_ref, l1_ref, acc1_ref), (m2_ref, l2_ref, acc2_ref)):
        m_ref[...] = jnp.full_like(m_ref, MASKED)
        l_ref[...] = jnp.zeros_like(l_ref)
        acc_ref[...] = jnp.zeros_like(acc_ref)

    maps = ((q1_ref, k1_ref, m1_ref, acc1_ref, l1_ref), (q2_ref, k2_ref, m2_ref, acc2_ref, l2_ref))

    def k_tile(k_ref, kt):
        return k_ref[0, 0, pl.ds(pl.multiple_of(kt * tk, tk), tk), :]

    def compute_scores(kt, buf):
        for (q_ref, k_ref, _, _, _), s_ref in zip(maps, buf):
            s_ref[...] = _dot_nt(k_tile(k_ref, kt), q_ref[0, 0])

    def consume(kt, buf):
        vt = vt_ref[0, 0, kt]
        for (_, _, m_ref, acc_ref, l_ref), s_ref in zip(maps, buf):
            _online_step(s_ref[...], vt, m_ref, acc_ref, l_ref)

    def compute_half(kt, buf):
        for (q_ref, k_ref, _, _, _), s_ref in zip(maps, buf):
            s_ref[:, :tk] = _dot_nt(k_tile(k_ref, kt), q_ref[0, 0, tk:, :])

    def consume_diag0(kt, buf):
        vt = vt_ref[0, 0, kt]
        for (_, _, m_ref, acc_ref, l_ref), s_ref in zip(maps, buf):
            s = jnp.concatenate([s_ref[:, :tk] + bias_ref[...], s_ref[:, tk:]], axis=1)
            _online_step(s, vt, m_ref, acc_ref, l_ref)

    def consume_diag1(kt, buf):
        vt = vt_ref[0, 0, kt]
        for (_, _, m_ref, acc_ref, l_ref), s_ref in zip(maps, buf):
            _online_step(s_ref[:, :tk] + bias_ref[...], vt, m_ref, acc_ref, l_ref,
                         lanes=slice(tk, None))

    _run_tiles(i, compute_scores, consume, compute_half, consume_diag0, consume_diag1,
               (s1a_ref, s2a_ref), (s1b_ref, s2b_ref))

    lams = lam_ref[...]
    lam = (jnp.exp(jnp.sum(lams[0:1] * lams[1:2], axis=-1, keepdims=True))
           - jnp.exp(jnp.sum(lams[2:3] * lams[3:4], axis=-1, keepdims=True)) + lam_init)
    o_t = acc1_ref[...] * (1.0 / l1_ref[...]) - lam * (acc2_ref[...] * (1.0 / l2_ref[...]))
    o_ref[0] = (_rms(o_t.T, g_ref[...]) * (1.0 - lam_init)).astype(BF16)


def _attn_specs(s, v_rows):
    tq, tk = TILE_Q, TILE_K
    q_spec = pl.BlockSpec((1, 1, tq, HEAD_PAD), lambda b, h, i: (b, h, i, 0))
    k_spec = pl.BlockSpec((1, 1, s, HEAD_PAD), lambda b, h, i: (b, h, 0, 0))
    vt_spec = pl.BlockSpec((1, 1, s // tk, v_rows, tk), lambda b, h, i: (b, h, 0, 0, 0))
    o_spec = pl.BlockSpec((1, tq, HEAD_PAD), lambda b, h, i: (b, i, h))
    return q_spec, k_spec, vt_spec, o_spec


def _single_map_attention(q, k, vt, *, chunked, name):
    b, _, s, _ = q.shape
    tq, tk = TILE_Q, TILE_K
    q_spec, k_spec, vt_spec, o_spec = _attn_specs(s, V_ROWS_ONES)
    score = pltpu.VMEM((tk, tq), F32)
    return pl.pallas_call(
        functools.partial(_single_map_kernel, tk=tk, chunked=chunked),
        grid=(b, HEADS, s // tq),
        in_specs=[q_spec, k_spec, vt_spec],
        out_specs=o_spec,
        out_shape=jax.ShapeDtypeStruct((b, s, _GROUP), BF16),
        scratch_shapes=[score, score, pltpu.VMEM((tk, tk), F32),
                        pltpu.VMEM((1, tq), F32), pltpu.VMEM((V_ROWS_ONES, tq), F32)],
        compiler_params=pltpu.CompilerParams(
            dimension_semantics=("arbitrary",) * 3, vmem_limit_bytes=VMEM_LIMIT),
        name=name,
    )(q, k, vt)


def _diff_attention(q1, q2, k1, k2, vt, lams, g, lam_init):
    b, _, s, _ = q1.shape
    tq, tk = TILE_Q, TILE_K
    q_spec, k_spec, vt_spec, o_spec = _attn_specs(s, HEAD_PAD)
    score = pltpu.VMEM((tk, tq), F32)
    stat = pltpu.VMEM((1, tq), F32)
    acc = pltpu.VMEM((HEAD_PAD, tq), F32)
    return pl.pallas_call(
        functools.partial(_diff_kernel, tk=tk, lam_init=lam_init),
        grid=(b, HEADS, s // tq),
        in_specs=[q_spec, q_spec, k_spec, k_spec, vt_spec,
                  pl.BlockSpec((4, QK_DIM), lambda b, h, i: (0, 0)),
                  pl.BlockSpec((1, HEAD_PAD), lambda b, h, i: (0, 0))],
        out_specs=o_spec,
        out_shape=jax.ShapeDtypeStruct((b, s, _GROUP), BF16),
        scratch_shapes=[score, score, score, score, pltpu.VMEM((tk, tk), F32),
                        stat, stat, acc, stat, stat, acc],
        compiler_params=pltpu.CompilerParams(
            dimension_semantics=("arbitrary",) * 3, vmem_limit_bytes=VMEM_LIMIT),
        name="diff_attn",
    )(q1, q2, k1, k2, vt, lams, g)


def _ffn_kernel(x_ref, of_ref, od_ref, om_ref, wof_ref, wod_ref, wom_ref, g2_ref,
                wup_ref, cw_ref, cb_ref, wdn_ref, gf_ref, o_ref, xn_ref, act_ref,
                *, tm, final):
    i = pl.program_id(1)

    @pl.when(i == 0)
    def _():
        xn_ref[0:HALO, :] = jnp.zeros((HALO, D_MODEL), BF16)

    @pl.when(i > 0)
    def _():
        xn_ref[0:HALO, :] = xn_ref[tm:tm + HALO, :]

    x1 = (x_ref[0] + _dot(of_ref[0], wof_ref[...]) + _dot(od_ref[0], wod_ref[...])
          + _dot(om_ref[0], wom_ref[...]))
    xn_ref[HALO:HALO + tm, :] = _rms(x1, g2_ref[...]).astype(BF16)

    def conv(up, off):
        cw = cw_ref[:, off:off + FF_CHUNK]
        y = cb_ref[:, off:off + FF_CHUNK]
        for j in range(CONV_W):
            lo = HALO - (CONV_W - 1) + j
            y = y + up[lo:lo + tm, :] * cw[j:j + 1, :]
        return y

    xn = xn_ref[...]
    for c in range(D_FF // FF_CHUNK):
        og = c * FF_CHUNK
        ou = D_FF + og
        yg = conv(_dot(xn, wup_ref[:, og:og + FF_CHUNK]), og)
        yu = conv(_dot(xn, wup_ref[:, ou:ou + FF_CHUNK]), ou)
        act = yg * (1.0 / (1.0 + jnp.exp(-yg))) * yu
        act_ref[:, og:og + FF_CHUNK] = act.astype(BF16)

    x2 = x1 + _dot(act_ref[...], wdn_ref[...])
    if final:
        x2 = _rms(x2, gf_ref[...])
    o_ref[0] = x2


def _ffn(x, o_fox, o_diff, o_mla, wo_f, wo_d, wo_m, g2, w_up, conv_w, conv_b, w_down, g_final,
         final):
    b, s, _ = x.shape
    tm = TILE_FFN
    row = lambda width: pl.BlockSpec((1, tm, width), lambda bi, i: (bi, i, 0))
    return pl.pallas_call(
        functools.partial(_ffn_kernel, tm=tm, final=final),
        grid=(b, s // tm),
        in_specs=[
            row(D_MODEL), row(_GROUP), row(_GROUP), row(_GROUP),
            _const_spec((_GROUP, D_MODEL)), _const_spec((_GROUP, D_MODEL)),
            _const_spec((_GROUP, D_MODEL)),
            _const_spec((1, D_MODEL)),
            _const_spec((D_MODEL, 2 * D_FF)),
            _const_spec((CONV_W, 2 * D_FF)),
            _const_spec((1, 2 * D_FF)),
            _const_spec((D_FF, D_MODEL)),
            _const_spec((1, D_MODEL)),
        ],
        out_specs=row(D_MODEL),
        out_shape=jax.ShapeDtypeStruct((b, s, D_MODEL), F32),
        scratch_shapes=[pltpu.VMEM((HALO + tm, D_MODEL), BF16), pltpu.VMEM((tm, D_FF), BF16)],
        compiler_params=pltpu.CompilerParams(
            dimension_semantics=("arbitrary", "arbitrary"), vmem_limit_bytes=VMEM_LIMIT),
        name="ffn",
    )(x, o_fox, o_diff, o_mla, wo_f, wo_d, wo_m, g2, w_up, conv_w, conv_b, w_down, g_final)


def _rot_half_cols(w):
    half = MLA_ROPE // 2
    return jnp.concatenate([w[..., half:], w[..., :half]], axis=-1)


def _layout_w_in(w):
    n_ff = OFF_FF + HEADS
    n_kr = w.shape[-1] - MLA_ROPE
    zeros = lambda n: jnp.zeros(w.shape[:-1] + (n,), w.dtype)
    kr = w[..., n_kr:]
    rope_pad = HEAD_PAD - QK_DIM - MLA_ROPE
    out = jnp.concatenate(
        [w[..., :n_ff], zeros(128 - HEADS), w[..., n_ff:n_kr],
         zeros(QK_DIM), kr, zeros(rope_pad),
         zeros(QK_DIM), _rot_half_cols(kr), zeros(rope_pad)], axis=-1).astype(BF16)
    assert out.shape[-1] == PROJ_COLS
    return out


def _layout_w_uq(w):
    w = w.reshape(DEPTH, MLA_Q_RANK, HEADS, QK_DIM + MLA_ROPE)
    nope, rope = w[..., :QK_DIM], w[..., QK_DIM:]
    pad = jnp.zeros(w.shape[:-1] + (HEAD_PAD - QK_DIM - MLA_ROPE,), w.dtype)
    direct = jnp.concatenate([nope, rope, pad], axis=-1)
    swapped = jnp.concatenate([jnp.zeros_like(nope), _rot_half_cols(rope), pad], axis=-1)
    return jnp.concatenate([direct.reshape(DEPTH, MLA_Q_RANK, -1),
                            swapped.reshape(DEPTH, MLA_Q_RANK, -1)], axis=-1).astype(BF16)


def _layout_w_ukv(w):
    w = w.reshape(DEPTH, MLA_KV_RANK, HEADS, 2 * QK_DIM)
    pad = jnp.zeros(w.shape[:-1] + (HEAD_PAD - QK_DIM,), w.dtype)
    k = jnp.concatenate([w[..., :QK_DIM], pad], axis=-1).reshape(DEPTH, MLA_KV_RANK, -1)
    v = jnp.concatenate([w[..., QK_DIM:], pad], axis=-1).reshape(DEPTH, MLA_KV_RANK, -1)
    return jnp.concatenate([k, v], axis=-1).astype(BF16)


def _pad_head_rows(w):
    w = w.reshape(DEPTH, HEADS, QK_DIM, -1)
    w = jnp.pad(w, ((0, 0), (0, 0), (0, HEAD_PAD - QK_DIM), (0, 0)))
    return w.reshape(DEPTH, HEADS * HEAD_PAD, -1)


def _rope_tables(s):
    half = MLA_ROPE // 2
    pos = jnp.arange(s, dtype=F32)
    inv = 1.0 / (ROPE_THETA ** (jnp.arange(half, dtype=F32) / half))
    ang = pos[:, None] * inv[None, :]
    cos, sin = jnp.cos(ang), jnp.sin(ang)
    tail = jnp.zeros((s, HEAD_PAD - QK_DIM - MLA_ROPE), F32)
    cos_t = jnp.concatenate([jnp.ones((s, QK_DIM), F32), cos, cos, tail], axis=1)
    sin_t = jnp.concatenate([jnp.zeros((s, QK_DIM), F32), -sin, sin, tail], axis=1)
    return cos_t, sin_t


def kernel(x, ln1_g, w_in, fgate_b, lam_q1, lam_k1, lam_q2, lam_k2, diff_norm_g, q_norm_g, w_uq,
           kv_norm_g, w_ukv, w_o, ln2_g, w_up, conv_w, conv_b, w_down, final_g):
    s = x.shape[1]
    cos_t, sin_t = _rope_tables(s)
    row = lambda v: v.reshape(DEPTH, 1, -1)
    wa = _layout_w_in(w_in)
    fb = row(jnp.pad(fgate_b, ((0, 0), (0, 128 - HEADS))))
    wuq = _layout_w_uq(w_uq)
    wukv = _layout_w_ukv(w_ukv)
    lams = jnp.stack([lam_q1, lam_k1, lam_q2, lam_k2], axis=1)
    wo = w_o.astype(BF16)
    wo_f = _pad_head_rows(wo[:, :_FOX])
    wo_d = wo[:, _FOX:_FOX + _GROUP]
    wo_m = _pad_head_rows(wo[:, _FOX + _GROUP:])
    wup = w_up.astype(BF16)
    wdn = w_down.astype(BF16)
    g1, g2, qng, kvng, dng, cb = (row(v) for v in (ln1_g, ln2_g, q_norm_g, kv_norm_g,
                                                   diff_norm_g, conv_b))
    for i in range(DEPTH):
        lam_init = 0.8 - 0.6 * math.exp(-0.3 * i)
        heads = _project(x, g1[i], wa[i], fb[i], qng[i], wuq[i], kvng[i], wukv[i], cos_t, sin_t)
        fq, fk, fvt, dq1, dq2, dk1, dk2, dvt, mq, mk, mvt = heads
        o_fox = _single_map_attention(fq, fk, fvt, chunked=False, name="fox_attn")
        o_diff = _diff_attention(dq1, dq2, dk1, dk2, dvt, lams[i], dng[i], lam_init)
        o_mla = _single_map_attention(mq, mk, mvt, chunked=True, name="mla_attn")
        x = _ffn(x, o_fox, o_diff, o_mla, wo_f[i], wo_d[i], wo_m[i], g2[i], wup[i], conv_w[i],
                 cb[i], wdn[i], final_g.reshape(1, -1), final=(i == DEPTH - 1))
    return x
```

```python
import functools
import math
from typing import NamedTuple

import jax
import jax.numpy as jnp
from jax import lax
from jax.experimental import pallas as pl
from jax.experimental.pallas import tpu as pltpu

F32 = jnp.float32
BF16 = jnp.bfloat16

D_MODEL = 1024
DEPTH = 2
CHUNK = 64
EPS = 1e-6
MASKED = -1e30

HEADS = 4
HEAD_PAD = 128
QK_DIM = 64
MLA_ROPE = 32
MLA_Q_RANK = 256
MLA_KV_RANK = 128
ROPE_THETA = 10000.0
D_FF = 2816
CONV_W = 3

LOG2E = 1.4426950408889634

_GROUP = HEADS * HEAD_PAD
_FOX = HEADS * QK_DIM
OFF_FQ, OFF_FK, OFF_FV = 0, _FOX, 2 * _FOX
OFF_FF = 3 * _FOX
OFF_DQ = OFF_FF + 128
OFF_DK = OFF_DQ + _GROUP
OFF_DV = OFF_DK + _GROUP
OFF_CQ = OFF_DV + _GROUP
OFF_CKV = OFF_CQ + MLA_Q_RANK
OFF_KR = OFF_CKV + MLA_KV_RANK
OFF_KRP = OFF_KR + 128
PROJ_COLS = OFF_KRP + 128

TILE_K = 512
TILE_Q = 2 * TILE_K
TILE_PROJ = TILE_K
BF16_ROWS = 16
V_ROWS_ONES = QK_DIM + BF16_ROWS
V_ROWS_DIFF = HEAD_PAD + BF16_ROWS
TILE_FFN = 512
FF_CHUNK = 256
HALO = 16
VMEM_LIMIT = 56 * 1024 * 1024


def _dot(a, b):
    return jnp.dot(a, b, preferred_element_type=F32)


def _dot_nt(a, b):
    return lax.dot_general(a, b, (((1,), (1,)), ((), ())), preferred_element_type=F32)


def _rms(x, g):
    return x * lax.rsqrt(jnp.mean(x * x, axis=-1, keepdims=True) + EPS) * g


def _split3(x):
    hi = x.astype(BF16)
    r1 = x - hi.astype(F32)
    mid = r1.astype(BF16)
    lo = (r1 - mid.astype(F32)).astype(BF16)
    return hi, mid, lo


def _const_spec(shape):
    nd = len(shape)
    return pl.BlockSpec(shape, lambda *_: (0,) * nd, pipeline_mode=pl.Buffered(1))


def _proj_kernel(x_ref, g1_ref, wa_ref, fb_ref, qng_ref, wuq_ref, kvng_ref, wukv_ref,
                 cos_ref, sin_ref,
                 fq_ref, fk_ref, fvt_ref, dq1_ref, dq2_ref, dk1_ref, dk2_ref, dvt_ref,
                 mq_ref, mk_ref, mvt_ref, carry_ref, *, tm):
    i = pl.program_id(1)

    @pl.when(i == 0)
    def _():
        carry_ref[...] = jnp.zeros_like(carry_ref)

    xn = _rms(x_ref[0], g1_ref[...]).astype(BF16)

    def proj(off, width):
        return _dot(xn, wa_ref[:, off:off + width])

    def head(t, h):
        return t[:, h * HEAD_PAD:(h + 1) * HEAD_PAD]

    lane = lax.broadcasted_iota(jnp.int32, (tm, HEAD_PAD), 1)
    low = lane < QK_DIM
    extra = jnp.where(low, lane, lane - QK_DIM)
    in_half = (low, jnp.logical_not(low))
    ones3 = tuple(jnp.where(jnp.logical_not(m) & (extra < 3), 1.0, 0.0) for m in in_half)
    one1 = tuple(jnp.where(jnp.logical_not(m) & (extra == 0), 1.0, 0.0) for m in in_half)
    qk_scale = QK_DIM ** -0.5 * LOG2E

    def v_rows(t, half):
        t = t.T
        if half == 0:
            return t[:V_ROWS_ONES]
        return jnp.concatenate([t[QK_DIM:], t[:V_ROWS_ONES - QK_DIM]], axis=0)

    z = proj(OFF_FF, 128) + fb_ref[...]
    logf = jnp.minimum(z, 0.0) - jnp.log1p(jnp.exp(-jnp.abs(z)))
    rr = lax.broadcasted_iota(jnp.int32, (tm, tm), 0)
    cc = lax.broadcasted_iota(jnp.int32, (tm, tm), 1)
    tri = (cc <= rr).astype(BF16)
    l_hi, l_mid, l_lo = _split3(logf)
    cum = _dot(tri, l_hi) + _dot(tri, l_mid) + _dot(tri, l_lo) + carry_ref[...]
    carry_ref[...] = cum[tm - 1:tm, :]
    b_hi, b_mid, b_lo = _split3(cum * (-LOG2E))
    pr = lax.broadcasted_iota(jnp.int32, (128, _GROUP), 0)
    pc = lax.broadcasted_iota(jnp.int32, (128, _GROUP), 1)
    place = pc - pr * HEAD_PAD - jnp.where(pr % 2 == 0, QK_DIM, 0)
    fk_bias = (_dot(b_hi, (place == 0).astype(BF16)) + _dot(b_mid, (place == 1).astype(BF16))
               + _dot(b_lo, (place == 2).astype(BF16)))
    hq = proj(OFF_FQ, _FOX)
    hk = proj(OFF_FK, _FOX)
    hv = proj(OFF_FV, _FOX)
    for h in range(HEADS):
        pair, half = divmod(h, 2)
        keep = in_half[half]
        fq_ref[0, h] = jnp.where(keep, head(hq, pair) * qk_scale, ones3[half]).astype(BF16)
        fk_ref[0, h] = jnp.where(keep, head(hk, pair), head(fk_bias, h)).astype(BF16)
        fvt_ref[0, h, 0] = v_rows(jnp.where(keep, head(hv, pair), one1[half]), half).astype(BF16)

    pos = (i * tm + lax.broadcasted_iota(jnp.int32, (tm, HEAD_PAD), 0)).astype(F32)
    hq = proj(OFF_DQ, _GROUP)
    hk = proj(OFF_DK, _GROUP)
    hv = proj(OFF_DV, _GROUP)
    ones_rows = jnp.where(lax.broadcasted_iota(jnp.int32, (BF16_ROWS, tm), 0) == 0, 1.0, 0.0)
    for h in range(HEADS):
        slope =2.0 ** (-8.0 * (h + 1) / HEADS)
        a_hi, a_mid, a_lo = _split3(pos * (slope * LOG2E))
        kbias = jnp.where(extra == 0, a_hi.astype(F32),
                          jnp.where(extra == 1, a_mid.astype(F32),
                                    jnp.where(extra == 2, a_lo.astype(F32), 0.0)))
        q = head(hq, h) * qk_scale
        k = head(hk, h)
        for half, (q_ref, k_ref) in enumerate(((dq1_ref, dk1_ref), (dq2_ref, dk2_ref))):
            q_ref[0, h] = jnp.where(in_half[half], q, ones3[half]).astype(BF16)
            k_ref[0, h] = jnp.where(in_half[half], k, kbias).astype(BF16)
        dvt_ref[0, h, 0] = jnp.concatenate([head(hv, h).T, ones_rows], axis=0).astype(BF16)

    cqn = _rms(proj(OFF_CQ, MLA_Q_RANK), qng_ref[...]).astype(BF16)
    ckvn = _rms(proj(OFF_CKV, MLA_KV_RANK), kvng_ref[...]).astype(BF16)
    cos = cos_ref[...]
    sin = sin_ref[...]
    krot = proj(OFF_KR, 128) * cos + proj(OFF_KRP, 128) * sin
    qa = _dot(cqn, wuq_ref[:, :_GROUP])
    qb = _dot(cqn, wuq_ref[:, _GROUP:])
    kk = _dot(ckvn, wukv_ref[:, :_GROUP])
    vv = _dot(ckvn, wukv_ref[:, _GROUP:])
    mla_scale = (QK_DIM + MLA_ROPE) ** -0.5 * LOG2E
    for h in range(HEADS):
        mq_ref[0, h] = ((head(qa, h) * cos + head(qb, h) * sin) * mla_scale).astype(BF16)
        mk_ref[0, h] = (head(kk, h) + krot).astype(BF16)
        mvt_ref[0, h, 0] = v_rows(head(vv, h) + one1[0], 0).astype(BF16)


def _project(x, g1, wa, fb, qng, wuq, kvng, wukv, cos_t, sin_t):
    b, s, _ = x.shape
    tm = TILE_PROJ
    rows = (jax.ShapeDtypeStruct((b, HEADS, s, HEAD_PAD), BF16),
            pl.BlockSpec((1, HEADS, tm, HEAD_PAD), lambda bi, i: (bi, 0, i, 0)))
    def cols(v_rows):
        return (jax.ShapeDtypeStruct((b, HEADS, s // tm, v_rows, tm), BF16),
                pl.BlockSpec((1, HEADS, 1, v_rows, tm), lambda bi, i: (bi, 0, i, 0, 0)))

    outs = [rows, rows, cols(V_ROWS_ONES), rows, rows, rows, rows, cols(V_ROWS_DIFF),
            rows, rows, cols(V_ROWS_ONES)]
    row_tab = pl.BlockSpec((tm, HEAD_PAD), lambda bi, i: (i, 0))
    return pl.pallas_call(
        functools.partial(_proj_kernel, tm=tm),
        grid=(b, s // tm),
        in_specs=[
            pl.BlockSpec((1, tm, D_MODEL), lambda bi, i: (bi, i, 0)),
            _const_spec((1, D_MODEL)),
            _const_spec((D_MODEL, PROJ_COLS)),
            _const_spec((1, 128)),
            _const_spec((1, MLA_Q_RANK)),
            _const_spec((MLA_Q_RANK, 2 * _GROUP)),
            _const_spec((1, MLA_KV_RANK)),
            _const_spec((MLA_KV_RANK, 2 * _GROUP)),
            row_tab, row_tab,
        ],
        out_specs=[o[1] for o in outs],
        out_shape=[o[0] for o in outs],
        scratch_shapes=[pltpu.VMEM((1, 128), F32)],
        compiler_params=pltpu.CompilerParams(
            dimension_semantics=("arbitrary", "arbitrary"), vmem_limit_bytes=VMEM_LIMIT),
        name="proj",
    )(x, g1, wa, fb, qng, wuq, kvng, wukv, cos_t, sin_t)


def _tile_ids(t):
    key = lax.broadcasted_iota(jnp.int32, (t, t), 0)
    qry = lax.broadcasted_iota(jnp.int32, (t, t), 1)
    return key, qry


def _online_step(s, vt, m_ref, acc_ref, lanes=slice(None)):
    m_prev = m_ref[:, lanes]
    m_new = jnp.maximum(m_prev, jnp.max(s, axis=0, keepdims=True))
    p = jnp.exp2(s - m_new)
    alpha = jnp.exp2(m_prev - m_new)
    acc_ref[:, lanes] = alpha * acc_ref[:, lanes] + _dot(vt, p.astype(BF16))
    m_ref[:, lanes] = m_new


class _Map(NamedTuple):
    q_ref: object
    k_ref: object
    vt_ref: object
    bias_ref: object
    m_ref: object
    acc_ref: object


def _flash(i, maps, buf_a, buf_b, tk):
    for mp in maps:
        mp.m_ref[...] = jnp.full_like(mp.m_ref, MASKED)
        mp.acc_ref[...] = jnp.zeros_like(mp.acc_ref)

    def k_tile(mp, kt):
        return mp.k_ref[0, 0, pl.ds(pl.multiple_of(kt * tk, tk), tk), :]

    def compute_scores(kt, buf):
        for mp, s_ref in zip(maps, buf):
            s_ref[...] = _dot_nt(k_tile(mp, kt), mp.q_ref[0, 0])

    def consume(kt, buf):
        for mp, s_ref in zip(maps, buf):
            _online_step(s_ref[...], mp.vt_ref[0, 0, kt], mp.m_ref, mp.acc_ref)

    compute_scores(0, buf_a)

    def pair(j):
        compute_scores(j + 1, buf_b)
        consume(j, buf_a)
        compute_scores(j + 2, buf_a)
        consume(j + 1, buf_b)

    odd = jnp.bitwise_and(i, 1)

    @pl.when(odd == 1)
    def _():
        pair(0)

    def body(jj, c):
        j = 2 * odd + 4 * jj
        pair(j)
        pair(j + 2)
        return c

    lax.fori_loop(0, lax.shift_right_logical(i, 1), body, 0)

    lower, upper = slice(0, tk), slice(tk, None)
    for mp, sb_ref in zip(maps, buf_b):
        sb_ref[:, lower] = _dot_nt(k_tile(mp, 2 * i + 1), mp.q_ref[0, 0, upper, :])
    for mp, sa_ref, sb_ref in zip(maps, buf_a, buf_b):
        bias = mp.bias_ref[...]
        vt0, vt1 = mp.vt_ref[0, 0, 2 * i], mp.vt_ref[0, 0, 2 * i + 1]
        _online_step(sa_ref[:, lower] + bias, vt0, mp.m_ref, mp.acc_ref, lanes=lower)
        s = jnp.concatenate([sa_ref[:, upper], sb_ref[:, lower] + bias], axis=0)
        _online_step(s, jnp.concatenate([vt0, vt1], axis=1), mp.m_ref, mp.acc_ref, lanes=upper)


def _normalized(acc_ref, rows):
    acc = acc_ref[...]
    return acc[:rows] * (1.0 / acc[rows:rows + 1, :])


def _diag_bias(tk, chunked, slope2=None):
    key, qry = _tile_ids(tk)
    keep = (key // CHUNK) <= (qry // CHUNK) if chunked else key <= qry
    if slope2 is None:
        return jnp.where(keep, 0.0, MASKED)
    return jnp.where(keep, jnp.maximum(key - qry, 0).astype(F32) * -slope2, MASKED)


def _fox_mla_kernel(fq_ref, fk_ref, fvt_ref, mq_ref, mk_ref, mvt_ref, of_ref, om_ref,
                    sfa_ref, sma_ref, sfb_ref, smb_ref, fbias_ref, mbias_ref,
                    fm_ref, facc_ref, mm_ref, macc_ref, *, tk):
    i = pl.program_id(2)

    @pl.when(i == 0)
    def _():
        fbias_ref[...] = _diag_bias(tk, False)
        mbias_ref[...] = _diag_bias(tk, True)

    maps = (_Map(fq_ref, fk_ref, fvt_ref, fbias_ref, fm_ref, facc_ref),
            _Map(mq_ref, mk_ref, mvt_ref, mbias_ref, mm_ref, macc_ref))
    _flash(i, maps, (sfa_ref, sma_ref), (sfb_ref, smb_ref), tk)

    for mp, o_ref in zip(maps, (of_ref, om_ref)):
        o_t = _normalized(mp.acc_ref, QK_DIM)
        o_t = jnp.concatenate([o_t, jnp.zeros((HEAD_PAD - QK_DIM, 2 * tk), F32)], axis=0)
        o_ref[0] = o_t.T.astype(BF16)


def _diff_kernel(q1_ref, q2_ref, k1_ref, k2_ref, vt_ref, lam_ref, g_ref, o_ref,
                 s1a_ref, s2a_ref, s1b_ref, s2b_ref, bias_ref,
                 m1_ref, acc1_ref, m2_ref, acc2_ref, *, tk, lam_init):
    h = pl.program_id(1)
    i = pl.program_id(2)

    @pl.when(i == 0)
    def _():
        slope2 = jnp.exp2(jnp.full((1, 1), -2.0, F32) * (h + 1).astype(F32)) * (2.0 * LOG2E)
        bias_ref[...] = _diag_bias(tk, True, slope2)

    maps = (_Map(q1_ref, k1_ref, vt_ref, bias_ref, m1_ref, acc1_ref),
            _Map(q2_ref, k2_ref, vt_ref, bias_ref, m2_ref, acc2_ref))
    _flash(i, maps, (s1a_ref, s2a_ref), (s1b_ref, s2b_ref), tk)

    lams = lam_ref[...]
    lam = (jnp.exp(jnp.sum(lams[0:1] * lams[1:2], axis=-1, keepdims=True))
           - jnp.exp(jnp.sum(lams[2:3] * lams[3:4], axis=-1, keepdims=True)) + lam_init)
    o_t = _normalized(acc1_ref, HEAD_PAD) - lam * _normalized(acc2_ref, HEAD_PAD)
    o_ref[0] = (_rms(o_t.T, g_ref[...]) * (1.0 - lam_init)).astype(BF16)


def _attn_specs(s, v_rows):
    tq, tk = TILE_Q, TILE_K
    q_spec = pl.BlockSpec((1, 1, tq, HEAD_PAD), lambda b, h, i: (b, h, i, 0))
    k_spec = pl.BlockSpec((1, 1, s, HEAD_PAD), lambda b, h, i: (b, h, 0, 0))
    vt_spec = pl.BlockSpec((1, 1, s // tk, v_rows, tk), lambda b, h, i: (b, h, 0, 0, 0))
    o_spec = pl.BlockSpec((1, tq, HEAD_PAD), lambda b, h, i: (b, i, h))
    return q_spec, k_spec, vt_spec, o_spec


def _fox_mla_attention(fq, fk, fvt, mq, mk, mvt):
    b, _, s, _ = fq.shape
    tq, tk = TILE_Q, TILE_K
    q_spec, k_spec, vt_spec, o_spec = _attn_specs(s, V_ROWS_ONES)
    score = pltpu.VMEM((tk, tq), F32)
    bias = pltpu.VMEM((tk, tk), F32)
    stat = pltpu.VMEM((1, tq), F32)
    acc = pltpu.VMEM((V_ROWS_ONES, tq), F32)
    out = jax.ShapeDtypeStruct((b, s, _GROUP), BF16)
    return pl.pallas_call(
        functools.partial(_fox_mla_kernel, tk=tk),
        grid=(b, HEADS, s // tq),
        in_specs=[q_spec, k_spec, vt_spec, q_spec, k_spec, vt_spec],
        out_specs=[o_spec, o_spec],
        out_shape=[out, out],
        scratch_shapes=[score, score, score, score, bias, bias, stat, acc, stat, acc],
        compiler_params=pltpu.CompilerParams(
            dimension_semantics=("arbitrary",) * 3, vmem_limit_bytes=VMEM_LIMIT),
        name="fox_mla_attn",
    )(fq, fk, fvt, mq, mk, mvt)


def _diff_attention(q1, q2, k1, k2, vt, lams, g, lam_init):
    b, _, s, _ = q1.shape
    tq, tk = TILE_Q, TILE_K
    q_spec, k_spec, vt_spec, o_spec = _attn_specs(s, V_ROWS_DIFF)
    score = pltpu.VMEM((tk, tq), F32)
    stat = pltpu.VMEM((1, tq), F32)
    acc = pltpu.VMEM((V_ROWS_DIFF, tq), F32)
    return pl.pallas_call(
        functools.partial(_diff_kernel, tk=tk, lam_init=lam_init),
        grid=(b, HEADS, s // tq),
        in_specs=[q_spec, q_spec, k_spec, k_spec, vt_spec,
                  pl.BlockSpec((4, QK_DIM), lambda b, h, i: (0, 0)),
                  pl.BlockSpec((1, HEAD_PAD), lambda b, h, i: (0, 0))],
        out_specs=o_spec,
        out_shape=jax.ShapeDtypeStruct((b, s, _GROUP), BF16),
        scratch_shapes=[score, score, score, score, pltpu.VMEM((tk, tk), F32),
                        stat, acc, stat, acc],
        compiler_params=pltpu.CompilerParams(
            dimension_semantics=("arbitrary",) * 3, vmem_limit_bytes=VMEM_LIMIT),
        name="diff_attn",
    )(q1, q2, k1, k2, vt, lams, g)


def _ffn_kernel(x_ref, of_ref, od_ref, om_ref, wof_ref, wod_ref, wom_ref, g2_ref,
                wup_ref, cw_ref, cb_ref, wdn_ref, gf_ref, o_ref, xn_ref, act_ref,
                *, tm, final):
    i = pl.program_id(1)

    @pl.when(i == 0)
    def _():
        xn_ref[0:HALO, :] = jnp.zeros((HALO, D_MODEL), BF16)

    @pl.when(i > 0)
    def _():
        xn_ref[0:HALO, :] = xn_ref[tm:tm + HALO, :]

    x1 = (x_ref[0] + _dot(of_ref[0], wof_ref[...]) + _dot(od_ref[0], wod_ref[...])
          + _dot(om_ref[0], wom_ref[...]))
    xn_ref[HALO:HALO + tm, :] = _rms(x1, g2_ref[...]).astype(BF16)

    def conv(up, off):
        cw = cw_ref[:, off:off + FF_CHUNK]
        y = cb_ref[:, off:off + FF_CHUNK]
        for j in range(CONV_W):
            lo = HALO - (CONV_W - 1) + j
            y = y + up[lo:lo + tm, :] * cw[j:j + 1, :]
        return y

    xn = xn_ref[...]
    for c in range(D_FF // FF_CHUNK):
        og = c * FF_CHUNK
        ou = D_FF + og
        yg = conv(_dot(xn, wup_ref[:, og:og + FF_CHUNK]), og)
        yu = conv(_dot(xn, wup_ref[:, ou:ou + FF_CHUNK]), ou)
        act = yg * (1.0 / (1.0 + jnp.exp(-yg))) * yu
        act_ref[:, og:og + FF_CHUNK] = act.astype(BF16)

    x2 = x1 + _dot(act_ref[...], wdn_ref[...])
    if final:
        x2 = _rms(x2, gf_ref[...])
    o_ref[0] = x2


def _ffn(x, o_fox, o_diff, o_mla, wo_f, wo_d, wo_m, g2, w_up, conv_w, conv_b, w_down, g_final,
         final):
    b, s, _ = x.shape
    tm = TILE_FFN
    row = lambda width: pl.BlockSpec((1, tm, width), lambda bi, i: (bi, i, 0))
    return pl.pallas_call(
        functools.partial(_ffn_kernel, tm=tm, final=final),
        grid=(b, s // tm),
        in_specs=[
            row(D_MODEL), row(_GROUP), row(_GROUP), row(_GROUP),
            _const_spec((_GROUP, D_MODEL)), _const_spec((_GROUP, D_MODEL)),
            _const_spec((_GROUP, D_MODEL)),
            _const_spec((1, D_MODEL)),
            _const_spec((D_MODEL, 2 * D_FF)),
            _const_spec((CONV_W, 2 * D_FF)),
            _const_spec((1, 2 * D_FF)),
            _const_spec((D_FF, D_MODEL)),
            _const_spec((1, D_MODEL)),
        ],
        out_specs=row(D_MODEL),
        out_shape=jax.ShapeDtypeStruct((b, s, D_MODEL), F32),
        scratch_shapes=[pltpu.VMEM((HALO + tm, D_MODEL), BF16), pltpu.VMEM((tm, D_FF), BF16)],
        compiler_params=pltpu.CompilerParams(
            dimension_semantics=("arbitrary", "arbitrary"), vmem_limit_bytes=VMEM_LIMIT),
        name="ffn",
    )(x, o_fox, o_diff, o_mla, wo_f, wo_d, wo_m, g2, w_up, conv_w, conv_b, w_down, g_final)


def _rot_half_cols(w):
    half = MLA_ROPE // 2
    return jnp.concatenate([w[..., half:], w[..., :half]], axis=-1)


def _layout_w_in(w):
    n_ff = OFF_FF + HEADS
    n_kr = w.shape[-1] - MLA_ROPE
    zeros = lambda n: jnp.zeros(w.shape[:-1] + (n,), w.dtype)
    kr = w[..., n_kr:]
    rope_pad = HEAD_PAD - QK_DIM - MLA_ROPE
    out = jnp.concatenate(
        [w[..., :n_ff], zeros(128 - HEADS), w[..., n_ff:n_kr],
         zeros(QK_DIM), kr, zeros(rope_pad),
         zeros(QK_DIM), _rot_half_cols(kr), zeros(rope_pad)], axis=-1).astype(BF16)
    assert out.shape[-1] == PROJ_COLS
    return out


def _layout_w_uq(w):
    w = w.reshape(DEPTH, MLA_Q_RANK, HEADS, QK_DIM + MLA_ROPE)
    nope, rope = w[..., :QK_DIM], w[..., QK_DIM:]
    pad = jnp.zeros(w.shape[:-1] + (HEAD_PAD - QK_DIM - MLA_ROPE,), w.dtype)
    direct = jnp.concatenate([nope, rope, pad], axis=-1)
    swapped = jnp.concatenate([jnp.zeros_like(nope), _rot_half_cols(rope), pad], axis=-1)
    return jnp.concatenate([direct.reshape(DEPTH, MLA_Q_RANK, -1),
                            swapped.reshape(DEPTH, MLA_Q_RANK, -1)], axis=-1).astype(BF16)


def _layout_w_ukv(w):
    w = w.reshape(DEPTH, MLA_KV_RANK, HEADS, 2 * QK_DIM)
    pad = jnp.zeros(w.shape[:-1] + (HEAD_PAD - QK_DIM,), w.dtype)
    k = jnp.concatenate([w[..., :QK_DIM], pad], axis=-1).reshape(DEPTH, MLA_KV_RANK, -1)
    v = jnp.concatenate([w[..., QK_DIM:], pad], axis=-1).reshape(DEPTH, MLA_KV_RANK, -1)
    return jnp.concatenate([k, v], axis=-1).astype(BF16)


def _pad_head_rows(w):
    w = w.reshape(DEPTH, HEADS, QK_DIM, -1)
    w = jnp.pad(w, ((0, 0), (0, 0), (0, HEAD_PAD - QK_DIM), (0, 0)))
    return w.reshape(DEPTH, HEADS * HEAD_PAD, -1)


def _rope_tables(s):
    half = MLA_ROPE // 2
    pos = jnp.arange(s, dtype=F32)
    inv = 1.0 / (ROPE_THETA ** (jnp.arange(half, dtype=F32) / half))
    ang = pos[:, None] * inv[None, :]
    cos, sin = jnp.cos(ang), jnp.sin(ang)
    tail = jnp.zeros((s, HEAD_PAD - QK_DIM - MLA_ROPE), F32)
    cos_t = jnp.concatenate([jnp.ones((s, QK_DIM), F32), cos, cos, tail], axis=1)
    sin_t = jnp.concatenate([jnp.zeros((s, QK_DIM), F32), -sin, sin, tail], axis=1)
    return cos_t, sin_t


def kernel(x, ln1_g, w_in, fgate_b, lam_q1, lam_k1, lam_q2, lam_k2, diff_norm_g, q_norm_g, w_uq,
           kv_norm_g, w_ukv, w_o, ln2_g, w_up, conv_w, conv_b, w_down, final_g):
    s = x.shape[1]
    cos_t, sin_t = _rope_tables(s)
    row = lambda v: v.reshape(DEPTH, 1, -1)
    wa = _layout_w_in(w_in)
    fb = row(jnp.pad(fgate_b, ((0, 0), (0, 128 - HEADS))))
    wuq = _layout_w_uq(w_uq)
    wukv = _layout_w_ukv(w_ukv)
    lams = jnp.stack([lam_q1, lam_k1, lam_q2, lam_k2], axis=1)
    wo = w_o.astype(BF16)
    wo_f = _pad_head_rows(wo[:, :_FOX])
    wo_d = wo[:, _FOX:_FOX + _GROUP]
    wo_m = _pad_head_rows(wo[:, _FOX + _GROUP:])
    wup = w_up.astype(BF16)
    wdn = w_down.astype(BF16)
    g1, g2, qng, kvng, dng, cb = (row(v) for v in (ln1_g, ln2_g, q_norm_g, kv_norm_g,
                                                   diff_norm_g, conv_b))
    for i in range(DEPTH):
        lam_init = 0.8 - 0.6 * math.exp(-0.3 * i)
        heads = _project(x, g1[i], wa[i], fb[i], qng[i], wuq[i], kvng[i], wukv[i], cos_t, sin_t)
        fq, fk, fvt, dq1, dq2, dk1, dk2, dvt, mq, mk, mvt = heads
        o_fox, o_mla = _fox_mla_attention(fq, fk, fvt, mq, mk, mvt)
        o_diff = _diff_attention(dq1, dq2, dk1, dk2, dvt, lams[i], dng[i], lam_init)
        x = _ffn(x, o_fox, o_diff, o_mla, wo_f[i], wo_d[i], wo_m[i], g2[i], wup[i], conv_w[i],
                 cb[i], wdn[i], final_g.reshape(1, -1), final=(i == DEPTH - 1))
    return x
```

```python
import functools
import math
from typing import NamedTuple

import jax
import jax.numpy as jnp
from jax import lax
from jax.experimental import pallas as pl
from jax.experimental.pallas import tpu as pltpu

F32 = jnp.float32
BF16 = jnp.bfloat16

D_MODEL = 1024
DEPTH = 2
CHUNK = 64
EPS = 1e-6
MASKED = -1e30

HEADS = 4
HEAD_PAD = 128
QK_DIM = 64
MLA_ROPE = 32
MLA_Q_RANK = 256
MLA_KV_RANK = 128
ROPE_THETA = 10000.0
D_FF = 2816
CONV_W = 3

LOG2E = 1.4426950408889634

_GROUP = HEADS * HEAD_PAD
_FOX = HEADS * QK_DIM
OFF_FQ, OFF_FK, OFF_FV = 0, _FOX, 2 * _FOX
OFF_FF = 3 * _FOX
OFF_DQ = OFF_FF + 128
OFF_DK = OFF_DQ + _GROUP
OFF_DV = OFF_DK + _GROUP
OFF_CQ = OFF_DV + _GROUP
OFF_CKV = OFF_CQ + MLA_Q_RANK
OFF_KR = OFF_CKV + MLA_KV_RANK
OFF_KRP = OFF_KR + 128
PROJ_COLS = OFF_KRP + 128

TILE_K = 512
TILE_Q = 2 * TILE_K
TILE_PROJ = TILE_K
BF16_ROWS = 16
V_ROWS_ONES = QK_DIM + BF16_ROWS
V_ROWS_DIFF = HEAD_PAD + BF16_ROWS
TILE_FFN = 512
FF_CHUNK = 256
HALO = 16
VMEM_LIMIT = 56 * 1024 * 1024


def _dot(a, b):
    return jnp.dot(a, b, preferred_element_type=F32)


def _dot_nt(a, b):
    return lax.dot_general(a, b, (((1,), (1,)), ((), ())), preferred_element_type=F32)


def _rms(x, g):
    return x * lax.rsqrt(jnp.mean(x * x, axis=-1, keepdims=True) + EPS) * g


def _split3(x):
    hi = x.astype(BF16)
    r1 = x - hi.astype(F32)
    mid = r1.astype(BF16)
    lo = (r1 - mid.astype(F32)).astype(BF16)
    return hi, mid, lo


def _const_spec(shape):
    nd = len(shape)
    return pl.BlockSpec(shape, lambda *_: (0,) * nd, pipeline_mode=pl.Buffered(1))


def _proj_kernel(x_ref, g1_ref, wa_ref, fb_ref, qng_ref, wuq_ref, kvng_ref, wukv_ref,
                 cos_ref, sin_ref,
                 fq_ref, fk_ref, fvt_ref, dq1_ref, dq2_ref, dk1_ref, dk2_ref, dvt_ref,
                 mq_ref, mk_ref, mvt_ref, carry_ref, *, tm):
    i = pl.program_id(1)

    @pl.when(i == 0)
    def _():
        carry_ref[...] = jnp.zeros_like(carry_ref)

    xn = _rms(x_ref[0], g1_ref[...]).astype(BF16)

    def proj(off, width):
        return _dot(xn, wa_ref[:, off:off + width])

    def head(t, h):
        return t[:, h * HEAD_PAD:(h + 1) * HEAD_PAD]

    lane = lax.broadcasted_iota(jnp.int32, (tm, HEAD_PAD), 1)
    low = lane < QK_DIM
    extra = jnp.where(low, lane, lane - QK_DIM)
    in_half = (low, jnp.logical_not(low))
    ones3 = tuple(jnp.where(jnp.logical_not(m) & (extra < 3), 1.0, 0.0) for m in in_half)
    one1 = tuple(jnp.where(jnp.logical_not(m) & (extra == 0), 1.0, 0.0) for m in in_half)
    qk_scale = QK_DIM ** -0.5 * LOG2E

    def v_rows(t, half):
        t = t.T
        if half == 0:
            return t[:V_ROWS_ONES]
        return jnp.concatenate([t[QK_DIM:], t[:V_ROWS_ONES - QK_DIM]], axis=0)

    z = proj(OFF_FF, 128) + fb_ref[...]
    logf = jnp.minimum(z, 0.0) - jnp.log1p(jnp.exp(-jnp.abs(z)))
    rr = lax.broadcasted_iota(jnp.int32, (tm, tm), 0)
    cc = lax.broadcasted_iota(jnp.int32, (tm, tm), 1)
    tri = (cc <= rr).astype(BF16)
    parts = _dot(tri, jnp.concatenate(_split3(logf), axis=1))
    cum = parts[:, :128] + parts[:, 128:256] + parts[:, 256:] + carry_ref[...]
    carry_ref[...] = cum[tm - 1:tm, :]
    pr = lax.broadcasted_iota(jnp.int32, (3 * 128, _GROUP), 0)
    pc = lax.broadcasted_iota(jnp.int32, (3 * 128, _GROUP), 1)
    term, hd = pr // 128, pr % 128
    place = pc - hd * HEAD_PAD - jnp.where(hd % 2 == 0, QK_DIM, 0) == term
    fk_bias = _dot(jnp.concatenate(_split3(cum * (-LOG2E)), axis=1), place.astype(BF16))
    hq = proj(OFF_FQ, _FOX)
    hk = proj(OFF_FK, _FOX)
    hv = proj(OFF_FV, _FOX)
    for h in range(HEADS):
        pair, half = divmod(h, 2)
        keep = in_half[half]
        fq_ref[0, h] = jnp.where(keep, head(hq, pair) * qk_scale, ones3[half]).astype(BF16)
        fk_ref[0, h] = jnp.where(keep, head(hk, pair), head(fk_bias, h)).astype(BF16)
        fvt_ref[0, h, 0] = v_rows(jnp.where(keep, head(hv, pair), one1[half]), half).astype(BF16)

    pos = (i * tm + lax.broadcasted_iota(jnp.int32, (tm, HEAD_PAD), 0)).astype(F32)
    hq = proj(OFF_DQ, _GROUP)
    hk = proj(OFF_DK, _GROUP)
    hv = proj(OFF_DV, _GROUP)
    ones_rows = jnp.where(lax.broadcasted_iota(jnp.int32, (BF16_ROWS, tm), 0) == 0, 1.0, 0.0)
    for h in range(HEADS):
        slope =2.0 ** (-8.0 * (h + 1) / HEADS)
        a_hi, a_mid, a_lo = _split3(pos * (slope * LOG2E))
        kbias = jnp.where(extra == 0, a_hi.astype(F32),
                          jnp.where(extra == 1, a_mid.astype(F32),
                                    jnp.where(extra == 2, a_lo.astype(F32), 0.0)))
        q = head(hq, h) * qk_scale
        k = head(hk, h)
        for half, (q_ref, k_ref) in enumerate(((dq1_ref, dk1_ref), (dq2_ref, dk2_ref))):
            q_ref[0, h] = jnp.where(in_half[half], q, ones3[half]).astype(BF16)
            k_ref[0, h] = jnp.where(in_half[half], k, kbias).astype(BF16)
        dvt_ref[0, h, 0] = jnp.concatenate([head(hv, h).T, ones_rows], axis=0).astype(BF16)

    cqn = _rms(proj(OFF_CQ, MLA_Q_RANK), qng_ref[...]).astype(BF16)
    ckvn = _rms(proj(OFF_CKV, MLA_KV_RANK), kvng_ref[...]).astype(BF16)
    cos = cos_ref[...]
    sin = sin_ref[...]
    krot = proj(OFF_KR, 128) * cos + proj(OFF_KRP, 128) * sin
    qa = _dot(cqn, wuq_ref[:, :_GROUP])
    qb = _dot(cqn, wuq_ref[:, _GROUP:])
    kk = _dot(ckvn, wukv_ref[:, :_GROUP])
    vv = _dot(ckvn, wukv_ref[:, _GROUP:])
    mla_scale = (QK_DIM + MLA_ROPE) ** -0.5 * LOG2E
    for h in range(HEADS):
        mq_ref[0, h] = ((head(qa, h) * cos + head(qb, h) * sin) * mla_scale).astype(BF16)
        mk_ref[0, h] = (head(kk, h) + krot).astype(BF16)
        mvt_ref[0, h, 0] = v_rows(head(vv, h) + one1[0], 0).astype(BF16)


def _project(x, g1, wa, fb, qng, wuq, kvng, wukv, cos_t, sin_t):
    b, s, _ = x.shape
    tm = TILE_PROJ
    rows = (jax.ShapeDtypeStruct((b, HEADS, s, HEAD_PAD), BF16),
            pl.BlockSpec((1, HEADS, tm, HEAD_PAD), lambda bi, i: (bi, 0, i, 0)))
    def cols(v_rows):
        return (jax.ShapeDtypeStruct((b, HEADS, s // tm, v_rows, tm), BF16),
                pl.BlockSpec((1, HEADS, 1, v_rows, tm), lambda bi, i: (bi, 0, i, 0, 0)))

    outs = [rows, rows, cols(V_ROWS_ONES), rows, rows, rows, rows, cols(V_ROWS_DIFF),
            rows, rows, cols(V_ROWS_ONES)]
    row_tab = pl.BlockSpec((tm, HEAD_PAD), lambda bi, i: (i, 0))
    return pl.pallas_call(
        functools.partial(_proj_kernel, tm=tm),
        grid=(b, s // tm),
        in_specs=[
            pl.BlockSpec((1, tm, D_MODEL), lambda bi, i: (bi, i, 0)),
            _const_spec((1, D_MODEL)),
            _const_spec((D_MODEL, PROJ_COLS)),
            _const_spec((1, 128)),
            _const_spec((1, MLA_Q_RANK)),
            _const_spec((MLA_Q_RANK, 2 * _GROUP)),
            _const_spec((1, MLA_KV_RANK)),
            _const_spec((MLA_KV_RANK, 2 * _GROUP)),
            row_tab, row_tab,
        ],
        out_specs=[o[1] for o in outs],
        out_shape=[o[0] for o in outs],
        scratch_shapes=[pltpu.VMEM((1, 128), F32)],
        compiler_params=pltpu.CompilerParams(
            dimension_semantics=("arbitrary", "arbitrary"), vmem_limit_bytes=VMEM_LIMIT),
        name="proj",
    )(x, g1, wa, fb, qng, wuq, kvng, wukv, cos_t, sin_t)


def _tile_ids(t):
    key = lax.broadcasted_iota(jnp.int32, (t, t), 0)
    qry = lax.broadcasted_iota(jnp.int32, (t, t), 1)
    return key, qry


def _online_step(s, vt, m_ref, acc_ref, lanes=slice(None)):
    m_prev = m_ref[:, lanes]
    m_new = jnp.maximum(m_prev, jnp.max(s, axis=0, keepdims=True))
    p = jnp.exp2(s - m_new)
    alpha = jnp.exp2(m_prev - m_new)
    acc_ref[:, lanes] = alpha * acc_ref[:, lanes] + _dot(vt, p.astype(BF16))
    m_ref[:, lanes] = m_new


class _Map(NamedTuple):
    q_ref: object
    k_ref: object
    vt_ref: object
    bias_ref: object
    m_ref: object
    acc_ref: object


def _flash(i, maps, buf_a, buf_b, tk):
    for mp in maps:
        mp.m_ref[...] = jnp.full_like(mp.m_ref, MASKED)
        mp.acc_ref[...] = jnp.zeros_like(mp.acc_ref)

    def k_tile(mp, kt):
        return mp.k_ref[0, 0, pl.ds(pl.multiple_of(kt * tk, tk), tk), :]

    def compute_scores(kt, buf):
        for mp, s_ref in zip(maps, buf):
            s_ref[...] = _dot_nt(k_tile(mp, kt), mp.q_ref[0, 0])

    def consume(kt, buf):
        for mp, s_ref in zip(maps, buf):
            _online_step(s_ref[...], mp.vt_ref[0, 0, kt], mp.m_ref, mp.acc_ref)

    compute_scores(0, buf_a)

    def pair(j):
        compute_scores(j + 1, buf_b)
        consume(j, buf_a)
        compute_scores(j + 2, buf_a)
        consume(j + 1, buf_b)

    odd = jnp.bitwise_and(i, 1)

    @pl.when(odd == 1)
    def _():
        pair(0)

    def body(jj, c):
        j = 2 * odd + 4 * jj
        pair(j)
        pair(j + 2)
        return c

    lax.fori_loop(0, lax.shift_right_logical(i, 1), body, 0)

    lower, upper = slice(0, tk), slice(tk, None)
    for mp, sb_ref in zip(maps, buf_b):
        sb_ref[:, lower] = _dot_nt(k_tile(mp, 2 * i + 1), mp.q_ref[0, 0, upper, :])
    for mp, sa_ref, sb_ref in zip(maps, buf_a, buf_b):
        bias = mp.bias_ref[...]
        vt0, vt1 = mp.vt_ref[0, 0, 2 * i], mp.vt_ref[0, 0, 2 * i + 1]
        _online_step(sa_ref[:, lower] + bias, vt0, mp.m_ref, mp.acc_ref, lanes=lower)
        s = jnp.concatenate([sa_ref[:, upper], sb_ref[:, lower] + bias], axis=0)
        _online_step(s, jnp.concatenate([vt0, vt1], axis=1), mp.m_ref, mp.acc_ref, lanes=upper)


def _normalized(acc_ref, rows):
    acc = acc_ref[...]
    return acc[:rows] * (1.0 / acc[rows:rows + 1, :])


def _diag_bias(tk, chunked, slope2=None):
    key, qry = _tile_ids(tk)
    keep = (key // CHUNK) <= (qry // CHUNK) if chunked else key <= qry
    if slope2 is None:
        return jnp.where(keep, 0.0, MASKED)
    return jnp.where(keep, jnp.maximum(key - qry, 0).astype(F32) * -slope2, MASKED)


def _fox_mla_kernel(fq_ref, fk_ref, fvt_ref, mq_ref, mk_ref, mvt_ref, o_ref,
                    sfa_ref, sma_ref, sfb_ref, smb_ref, fbias_ref, mbias_ref,
                    fm_ref, facc_ref, mm_ref, macc_ref, *, tk):
    i = pl.program_id(2)

    @pl.when(i == 0)
    def _():
        fbias_ref[...] = _diag_bias(tk, False)
        mbias_ref[...] = _diag_bias(tk, True)

    maps = (_Map(fq_ref, fk_ref, fvt_ref, fbias_ref, fm_ref, facc_ref),
            _Map(mq_ref, mk_ref, mvt_ref, mbias_ref, mm_ref, macc_ref))
    _flash(i, maps, (sfa_ref, sma_ref), (sfb_ref, smb_ref), tk)

    o_t = jnp.concatenate([_normalized(mp.acc_ref, QK_DIM) for mp in maps], axis=0)
    o_ref[0] = o_t.T.astype(BF16)


def _diff_kernel(q1_ref, q2_ref, k1_ref, k2_ref, vt_ref, lam_ref, g_ref, o_ref,
                 s1a_ref, s2a_ref, s1b_ref, s2b_ref, bias_ref,
                 m1_ref, acc1_ref, m2_ref, acc2_ref, *, tk, lam_init):
    h = pl.program_id(1)
    i = pl.program_id(2)

    @pl.when(i == 0)
    def _():
        slope2 = jnp.exp2(jnp.full((1, 1), -2.0, F32) * (h + 1).astype(F32)) * (2.0 * LOG2E)
        bias_ref[...] = _diag_bias(tk, True, slope2)

    maps = (_Map(q1_ref, k1_ref, vt_ref, bias_ref, m1_ref, acc1_ref),
            _Map(q2_ref, k2_ref, vt_ref, bias_ref, m2_ref, acc2_ref))
    _flash(i, maps, (s1a_ref, s2a_ref), (s1b_ref, s2b_ref), tk)

    lams = lam_ref[...]
    lam = (jnp.exp(jnp.sum(lams[0:1] * lams[1:2], axis=-1, keepdims=True))
           - jnp.exp(jnp.sum(lams[2:3] * lams[3:4], axis=-1, keepdims=True)) + lam_init)
    o_t = _normalized(acc1_ref, HEAD_PAD) - lam * _normalized(acc2_ref, HEAD_PAD)
    o_ref[0] = (_rms(o_t.T, g_ref[...]) * (1.0 - lam_init)).astype(BF16)


def _attn_specs(s, v_rows):
    tq, tk = TILE_Q, TILE_K
    q_spec = pl.BlockSpec((1, 1, tq, HEAD_PAD), lambda b, h, i: (b, h, i, 0))
    k_spec = pl.BlockSpec((1, 1, s, HEAD_PAD), lambda b, h, i: (b, h, 0, 0))
    vt_spec = pl.BlockSpec((1, 1, s // tk, v_rows, tk), lambda b, h, i: (b, h, 0, 0, 0))
    o_spec = pl.BlockSpec((1, tq, HEAD_PAD), lambda b, h, i: (b, i, h))
    return q_spec, k_spec, vt_spec, o_spec


def _fox_mla_attention(fq, fk, fvt, mq, mk, mvt):
    b, _, s, _ = fq.shape
    tq, tk = TILE_Q, TILE_K
    q_spec, k_spec, vt_spec, o_spec = _attn_specs(s, V_ROWS_ONES)
    score = pltpu.VMEM((tk, tq), F32)
    bias = pltpu.VMEM((tk, tk), F32)
    stat = pltpu.VMEM((1, tq), F32)
    acc = pltpu.VMEM((V_ROWS_ONES, tq), F32)
    return pl.pallas_call(
        functools.partial(_fox_mla_kernel, tk=tk),
        grid=(b, HEADS, s // tq),
        in_specs=[q_spec, k_spec, vt_spec, q_spec, k_spec, vt_spec],
        out_specs=o_spec,
        out_shape=jax.ShapeDtypeStruct((b, s, _GROUP), BF16),
        scratch_shapes=[score, score, score, score, bias, bias, stat, acc, stat, acc],
        compiler_params=pltpu.CompilerParams(
            dimension_semantics=("arbitrary",) * 3, vmem_limit_bytes=VMEM_LIMIT),
        name="fox_mla_attn",
    )(fq, fk, fvt, mq, mk, mvt)


def _diff_attention(q1, q2, k1, k2, vt, lams, g, lam_init):
    b, _, s, _ = q1.shape
    tq, tk = TILE_Q, TILE_K
    q_spec, k_spec, vt_spec, o_spec = _attn_specs(s, V_ROWS_DIFF)
    score = pltpu.VMEM((tk, tq), F32)
    stat = pltpu.VMEM((1, tq), F32)
    acc = pltpu.VMEM((V_ROWS_DIFF, tq), F32)
    return pl.pallas_call(
        functools.partial(_diff_kernel, tk=tk, lam_init=lam_init),
        grid=(b, HEADS, s // tq),
        in_specs=[q_spec, q_spec, k_spec, k_spec, vt_spec,
                  pl.BlockSpec((4, QK_DIM), lambda b, h, i: (0, 0)),
                  pl.BlockSpec((1, HEAD_PAD), lambda b, h, i: (0, 0))],
        out_specs=o_spec,
        out_shape=jax.ShapeDtypeStruct((b, s, _GROUP), BF16),
        scratch_shapes=[score, score, score, score, pltpu.VMEM((tk, tk), F32),
                        stat, acc, stat, acc],
        compiler_params=pltpu.CompilerParams(
            dimension_semantics=("arbitrary",) * 3, vmem_limit_bytes=VMEM_LIMIT),
        name="diff_attn",
    )(q1, q2, k1, k2, vt, lams, g)


def _ffn_kernel(x_ref, ofm_ref, od_ref, wofm_ref, wod_ref, g2_ref,
                wup_ref, cw_ref, cb_ref, wdn_ref, gf_ref, o_ref, xn_ref, act_ref,
                *, tm, final):
    i = pl.program_id(1)

    @pl.when(i == 0)
    def _():
        xn_ref[0:HALO, :] = jnp.zeros((HALO, D_MODEL), BF16)

    @pl.when(i > 0)
    def _():
        xn_ref[0:HALO, :] = xn_ref[tm:tm + HALO, :]

    x1 = x_ref[0] + _dot(ofm_ref[0], wofm_ref[...]) + _dot(od_ref[0], wod_ref[...])
    xn_ref[HALO:HALO + tm, :] = _rms(x1, g2_ref[...]).astype(BF16)

    def conv(up, off):
        cw = cw_ref[:, off:off + FF_CHUNK]
        y = cb_ref[:, off:off + FF_CHUNK]
        for j in range(CONV_W):
            lo = HALO - (CONV_W - 1) + j
            y = y + up[lo:lo + tm, :] * cw[j:j + 1, :]
        return y

    xn = xn_ref[...]
    for c in range(D_FF // FF_CHUNK):
        og = c * FF_CHUNK
        ou = D_FF + og
        yg = conv(_dot(xn, wup_ref[:, og:og + FF_CHUNK]), og)
        yu = conv(_dot(xn, wup_ref[:, ou:ou + FF_CHUNK]), ou)
        act = yg * (1.0 / (1.0 + jnp.exp(-yg))) * yu
        act_ref[:, og:og + FF_CHUNK] = act.astype(BF16)

    x2 = x1 + _dot(act_ref[...], wdn_ref[...])
    if final:
        x2 = _rms(x2, gf_ref[...])
    o_ref[0] = x2


def _ffn(x, o_fm, o_diff, wo_fm, wo_d, g2, w_up, conv_w, conv_b, w_down, g_final, final):
    b, s, _ = x.shape
    tm = TILE_FFN
    row = lambda width: pl.BlockSpec((1, tm, width), lambda bi, i: (bi, i, 0))
    return pl.pallas_call(
        functools.partial(_ffn_kernel, tm=tm, final=final),
        grid=(b, s // tm),
        in_specs=[
            row(D_MODEL), row(_GROUP), row(_GROUP),
            _const_spec((_GROUP, D_MODEL)), _const_spec((_GROUP, D_MODEL)),
            _const_spec((1, D_MODEL)),
            _const_spec((D_MODEL, 2 * D_FF)),
            _const_spec((CONV_W, 2 * D_FF)),
            _const_spec((1, 2 * D_FF)),
            _const_spec((D_FF, D_MODEL)),
            _const_spec((1, D_MODEL)),
        ],
        out_specs=row(D_MODEL),
        out_shape=jax.ShapeDtypeStruct((b, s, D_MODEL), F32),
        scratch_shapes=[pltpu.VMEM((HALO + tm, D_MODEL), BF16), pltpu.VMEM((tm, D_FF), BF16)],
        compiler_params=pltpu.CompilerParams(
            dimension_semantics=("arbitrary", "arbitrary"), vmem_limit_bytes=VMEM_LIMIT),
        name="ffn",
    )(x, o_fm, o_diff, wo_fm, wo_d, g2, w_up, conv_w, conv_b, w_down, g_final)


def _rot_half_cols(w):
    half = MLA_ROPE // 2
    return jnp.concatenate([w[..., half:], w[..., :half]], axis=-1)


def _layout_w_in(w):
    n_ff = OFF_FF + HEADS
    n_kr = w.shape[-1] - MLA_ROPE
    zeros = lambda n: jnp.zeros(w.shape[:-1] + (n,), w.dtype)
    kr = w[..., n_kr:]
    rope_pad = HEAD_PAD - QK_DIM - MLA_ROPE
    out = jnp.concatenate(
        [w[..., :n_ff], zeros(128 - HEADS), w[..., n_ff:n_kr],
         zeros(QK_DIM), kr, zeros(rope_pad),
         zeros(QK_DIM), _rot_half_cols(kr), zeros(rope_pad)], axis=-1).astype(BF16)
    assert out.shape[-1] == PROJ_COLS
    return out


def _layout_w_uq(w):
    w = w.reshape(DEPTH, MLA_Q_RANK, HEADS, QK_DIM + MLA_ROPE)
    nope, rope = w[..., :QK_DIM], w[..., QK_DIM:]
    pad = jnp.zeros(w.shape[:-1] + (HEAD_PAD - QK_DIM - MLA_ROPE,), w.dtype)
    direct = jnp.concatenate([nope, rope, pad], axis=-1)
    swapped = jnp.concatenate([jnp.zeros_like(nope), _rot_half_cols(rope), pad], axis=-1)
    return jnp.concatenate([direct.reshape(DEPTH, MLA_Q_RANK, -1),
                            swapped.reshape(DEPTH, MLA_Q_RANK, -1)], axis=-1).astype(BF16)


def _layout_w_ukv(w):
    w = w.reshape(DEPTH, MLA_KV_RANK, HEADS, 2 * QK_DIM)
    pad = jnp.zeros(w.shape[:-1] + (HEAD_PAD - QK_DIM,), w.dtype)
    k = jnp.concatenate([w[..., :QK_DIM], pad], axis=-1).reshape(DEPTH, MLA_KV_RANK, -1)
    v = jnp.concatenate([w[..., QK_DIM:], pad], axis=-1).reshape(DEPTH, MLA_KV_RANK, -1)
    return jnp.concatenate([k, v], axis=-1).astype(BF16)


def _interleave_head_rows(w_fox, w_mla):
    parts = [w.reshape(DEPTH, HEADS, QK_DIM, -1) for w in (w_fox, w_mla)]
    return jnp.concatenate(parts, axis=2).reshape(DEPTH, HEADS * HEAD_PAD, -1)


def _rope_tables(s):
    half = MLA_ROPE // 2
    pos = jnp.arange(s, dtype=F32)
    inv = 1.0 / (ROPE_THETA ** (jnp.arange(half, dtype=F32) / half))
    ang = pos[:, None] * inv[None, :]
    cos, sin = jnp.cos(ang), jnp.sin(ang)
    tail = jnp.zeros((s, HEAD_PAD - QK_DIM - MLA_ROPE), F32)
    cos_t = jnp.concatenate([jnp.ones((s, QK_DIM), F32), cos, cos, tail], axis=1)
    sin_t = jnp.concatenate([jnp.zeros((s, QK_DIM), F32), -sin, sin, tail], axis=1)
    return cos_t, sin_t


def kernel(x, ln1_g, w_in, fgate_b, lam_q1, lam_k1, lam_q2, lam_k2, diff_norm_g, q_norm_g, w_uq,
           kv_norm_g, w_ukv, w_o, ln2_g, w_up, conv_w, conv_b, w_down, final_g):
    s = x.shape[1]
    cos_t, sin_t = _rope_tables(s)
    row = lambda v: v.reshape(DEPTH, 1, -1)
    wa = _layout_w_in(w_in)
    fb = row(jnp.pad(fgate_b, ((0, 0), (0, 128 - HEADS))))
    wuq = _layout_w_uq(w_uq)
    wukv = _layout_w_ukv(w_ukv)
    lams = jnp.stack([lam_q1, lam_k1, lam_q2, lam_k2], axis=1)
    wo = w_o.astype(BF16)
    wo_fm = _interleave_head_rows(wo[:, :_FOX], wo[:, _FOX + _GROUP:])
    wo_d = wo[:, _FOX:_FOX + _GROUP]
    wup = w_up.astype(BF16)
    wdn = w_down.astype(BF16)
    g1, g2, qng, kvng, dng, cb = (row(v) for v in (ln1_g, ln2_g, q_norm_g, kv_norm_g,
                                                   diff_norm_g, conv_b))
    for i in range(DEPTH):
        lam_init = 0.8 - 0.6 * math.exp(-0.3 * i)
        heads = _project(x, g1[i], wa[i], fb[i], qng[i], wuq[i], kvng[i], wukv[i], cos_t, sin_t)
        fq, fk, fvt, dq1, dq2, dk1, dk2, dvt, mq, mk, mvt = heads
        o_fm = _fox_mla_attention(fq, fk, fvt, mq, mk, mvt)
        o_diff = _diff_attention(dq1, dq2, dk1, dk2, dvt, lams[i], dng[i], lam_init)
        x = _ffn(x, o_fm, o_diff, wo_fm[i], wo_d[i], g2[i], wup[i], conv_w[i],
                 cb[i], wdn[i], final_g.reshape(1, -1), final=(i == DEPTH - 1))
    return x
```

```python
import functools
import math
from typing import NamedTuple

import jax
import jax.numpy as jnp
from jax import lax
from jax.experimental import pallas as pl
from jax.experimental.pallas import tpu as pltpu

F32 = jnp.float32
BF16 = jnp.bfloat16

D_MODEL = 1024
DEPTH = 2
CHUNK = 64
EPS = 1e-6
MASKED = -1e30

HEADS = 4
HEAD_PAD = 128
QK_DIM = 64
MLA_ROPE = 32
MLA_Q_RANK = 256
MLA_KV_RANK = 128
ROPE_THETA = 10000.0
D_FF = 2816
CONV_W = 3

LOG2E = 1.4426950408889634

_GROUP = HEADS * HEAD_PAD
_FOX = HEADS * QK_DIM
OFF_FQ, OFF_FK, OFF_FV = 0, _FOX, 2 * _FOX
OFF_FF = 3 * _FOX
OFF_DQ = OFF_FF + 128
OFF_DK = OFF_DQ + _GROUP
OFF_DV = OFF_DK + _GROUP
OFF_CQ = OFF_DV + _GROUP
OFF_CKV = OFF_CQ + MLA_Q_RANK
OFF_KR = OFF_CKV + MLA_KV_RANK
OFF_KRP = OFF_KR + 128
PROJ_COLS = OFF_KRP + 128

TILE_K = 512
TILE_Q = 2 * TILE_K
TILE_PROJ = TILE_K
BF16_ROWS = 16
V_ROWS_ONES = QK_DIM + BF16_ROWS
V_ROWS_DIFF = HEAD_PAD + BF16_ROWS
TILE_FFN = 512
FF_CHUNK = 256
HALO = 16
VMEM_LIMIT = 56 * 1024 * 1024


def _dot(a, b):
    return jnp.dot(a, b, preferred_element_type=F32)


def _dot_nt(a, b):
    return lax.dot_general(a, b, (((1,), (1,)), ((), ())), preferred_element_type=F32)


def _rms(x, g):
    return x * lax.rsqrt(jnp.mean(x * x, axis=-1, keepdims=True) + EPS) * g


def _split3(x):
    hi = x.astype(BF16)
    r1 = x - hi.astype(F32)
    mid = r1.astype(BF16)
    lo = (r1 - mid.astype(F32)).astype(BF16)
    return hi, mid, lo


def _const_spec(shape):
    nd = len(shape)
    return pl.BlockSpec(shape, lambda *_: (0,) * nd, pipeline_mode=pl.Buffered(1))


def _proj_kernel(x_ref, g1_ref, wa_ref, fb_ref, qng_ref, wuq_ref, kvng_ref, wukv_ref,
                 cos_ref, sin_ref,
                 fq_ref, fk_ref, fvt_ref, dq1_ref, dq2_ref, dk1_ref, dk2_ref, dvt_ref,
                 mq_ref, mk_ref, mvt_ref, carry_ref, *, tm):
    i = pl.program_id(1)

    @pl.when(i == 0)
    def _():
        carry_ref[...] = jnp.zeros_like(carry_ref)

    xn = _rms(x_ref[0], g1_ref[...]).astype(BF16)

    def proj(off, width):
        return _dot(xn, wa_ref[:, off:off + width])

    def head(t, h):
        return t[:, h * HEAD_PAD:(h + 1) * HEAD_PAD]

    lane = lax.broadcasted_iota(jnp.int32, (tm, HEAD_PAD), 1)
    low = lane < QK_DIM
    extra = jnp.where(low, lane, lane - QK_DIM)
    in_half = (low, jnp.logical_not(low))
    ones3 = tuple(jnp.where(jnp.logical_not(m) & (extra < 3), 1.0, 0.0) for m in in_half)
    one1 = tuple(jnp.where(jnp.logical_not(m) & (extra == 0), 1.0, 0.0) for m in in_half)
    qk_scale = QK_DIM ** -0.5 * LOG2E

    def v_rows(t, half):
        t = t.T
        if half == 0:
            return t[:V_ROWS_ONES]
        return jnp.concatenate([t[QK_DIM:], t[:V_ROWS_ONES - QK_DIM]], axis=0)

    z = proj(OFF_FF, 128) + fb_ref[...]
    logf = jnp.minimum(z, 0.0) - jnp.log1p(jnp.exp(-jnp.abs(z)))
    rr = lax.broadcasted_iota(jnp.int32, (tm, tm), 0)
    cc = lax.broadcasted_iota(jnp.int32, (tm, tm), 1)
    tri = (cc <= rr).astype(BF16)
    parts = _dot(tri, jnp.concatenate(_split3(logf), axis=1))
    cum = parts[:, :128] + parts[:, 128:256] + parts[:, 256:] + carry_ref[...]
    carry_ref[...] = cum[tm - 1:tm, :]
    pr = lax.broadcasted_iota(jnp.int32, (3 * 128, _GROUP), 0)
    pc = lax.broadcasted_iota(jnp.int32, (3 * 128, _GROUP), 1)
    term, hd = pr // 128, pr % 128
    place = pc - hd * HEAD_PAD - jnp.where(hd % 2 == 0, QK_DIM, 0) == term
    fk_bias = _dot(jnp.concatenate(_split3(cum * (-LOG2E)), axis=1), place.astype(BF16))
    hq = proj(OFF_FQ, _FOX)
    hk = proj(OFF_FK, _FOX)
    hv = proj(OFF_FV, _FOX)
    for h in range(HEADS):
        pair, half = divmod(h, 2)
        keep = in_half[half]
        fq_ref[0, h] = jnp.where(keep, head(hq, pair) * qk_scale, ones3[half]).astype(BF16)
        fk_ref[0, h] = jnp.where(keep, head(hk, pair), head(fk_bias, h)).astype(BF16)
        fvt_ref[0, h, 0] = v_rows(jnp.where(keep, head(hv, pair), one1[half]), half).astype(BF16)

    pos = (i * tm + lax.broadcasted_iota(jnp.int32, (tm, HEAD_PAD), 0)).astype(F32)
    hq = proj(OFF_DQ, _GROUP)
    hk = proj(OFF_DK, _GROUP)
    hv = proj(OFF_DV, _GROUP)
    ones_rows = jnp.where(lax.broadcasted_iota(jnp.int32, (BF16_ROWS, tm), 0) == 0, 1.0, 0.0)
    for h in range(HEADS):
        slope =2.0 ** (-8.0 * (h + 1) / HEADS)
        a_hi, a_mid, a_lo = _split3(pos * (slope * LOG2E))
        kbias = jnp.where(extra == 0, a_hi.astype(F32),
                          jnp.where(extra == 1, a_mid.astype(F32),
                                    jnp.where(extra == 2, a_lo.astype(F32), 0.0)))
        q = head(hq, h) * qk_scale
        k = head(hk, h)
        for half, (q_ref, k_ref) in enumerate(((dq1_ref, dk1_ref), (dq2_ref, dk2_ref))):
            q_ref[0, h] = jnp.where(in_half[half], q, ones3[half]).astype(BF16)
            k_ref[0, h] = jnp.where(in_half[half], k, kbias).astype(BF16)
        dvt_ref[0, h, 0] = jnp.concatenate([head(hv, h).T, ones_rows], axis=0).astype(BF16)

    cqn = _rms(proj(OFF_CQ, MLA_Q_RANK), qng_ref[...]).astype(BF16)
    ckvn = _rms(proj(OFF_CKV, MLA_KV_RANK), kvng_ref[...]).astype(BF16)
    cos = cos_ref[...]
    sin = sin_ref[...]
    krot = proj(OFF_KR, 128) * cos + proj(OFF_KRP, 128) * sin
    qa = _dot(cqn, wuq_ref[:, :_GROUP])
    qb = _dot(cqn, wuq_ref[:, _GROUP:])
    kk = _dot(ckvn, wukv_ref[:, :_GROUP])
    vv = _dot(ckvn, wukv_ref[:, _GROUP:])
    mla_scale = (QK_DIM + MLA_ROPE) ** -0.5 * LOG2E
    for h in range(HEADS):
        mq_ref[0, h] = ((head(qa, h) * cos + head(qb, h) * sin) * mla_scale).astype(BF16)
        mk_ref[0, h] = (head(kk, h) + krot).astype(BF16)
        mvt_ref[0, h, 0] = v_rows(head(vv, h) + one1[0], 0).astype(BF16)


def _project(x, g1, wa, fb, qng, wuq, kvng, wukv, cos_t, sin_t):
    b, s, _ = x.shape
    tm = TILE_PROJ
    rows = (jax.ShapeDtypeStruct((b, HEADS, s, HEAD_PAD), BF16),
            pl.BlockSpec((1, HEADS, tm, HEAD_PAD), lambda bi, i: (bi, 0, i, 0)))
    def cols(v_rows):
        return (jax.ShapeDtypeStruct((b, HEADS, s // tm, v_rows, tm), BF16),
                pl.BlockSpec((1, HEADS, 1, v_rows, tm), lambda bi, i: (bi, 0, i, 0, 0)))

    outs = [rows, rows, cols(V_ROWS_ONES), rows, rows, rows, rows, cols(V_ROWS_DIFF),
            rows, rows, cols(V_ROWS_ONES)]
    row_tab = pl.BlockSpec((tm, HEAD_PAD), lambda bi, i: (i, 0))
    return pl.pallas_call(
        functools.partial(_proj_kernel, tm=tm),
        grid=(b, s // tm),
        in_specs=[
            pl.BlockSpec((1, tm, D_MODEL), lambda bi, i: (bi, i, 0)),
            _const_spec((1, D_MODEL)),
            _const_spec((D_MODEL, PROJ_COLS)),
            _const_spec((1, 128)),
            _const_spec((1, MLA_Q_RANK)),
            _const_spec((MLA_Q_RANK, 2 * _GROUP)),
            _const_spec((1, MLA_KV_RANK)),
            _const_spec((MLA_KV_RANK, 2 * _GROUP)),
            row_tab, row_tab,
        ],
        out_specs=[o[1] for o in outs],
        out_shape=[o[0] for o in outs],
        scratch_shapes=[pltpu.VMEM((1, 128), F32)],
        compiler_params=pltpu.CompilerParams(
            dimension_semantics=("arbitrary", "arbitrary"), vmem_limit_bytes=VMEM_LIMIT),
        name="proj",
    )(x, g1, wa, fb, qng, wuq, kvng, wukv, cos_t, sin_t)


def _tile_ids(t):
    key = lax.broadcasted_iota(jnp.int32, (t, t), 0)
    qry = lax.broadcasted_iota(jnp.int32, (t, t), 1)
    return key, qry


def _online_step(s, vt, m_ref, acc_ref, lanes=slice(None)):
    m_prev = m_ref[:, lanes]
    m_new = jnp.maximum(m_prev, jnp.max(s, axis=0, keepdims=True))
    p = jnp.exp2(s - m_new)
    alpha = jnp.exp2(m_prev - m_new)
    acc_ref[:, lanes] = alpha * acc_ref[:, lanes] + _dot(vt, p.astype(BF16))
    m_ref[:, lanes] = m_new


class _Map(NamedTuple):
    q_ref: object
    q_next_ref: object
    k_ref: object
    vt_ref: object
    bias_ref: object
    m_ref: object
    acc_ref: object


def _flash(i, maps, buf_a, buf_b, tk):
    for mp in maps:
        mp.m_ref[...] = jnp.full_like(mp.m_ref, MASKED)
        mp.acc_ref[...] = jnp.zeros_like(mp.acc_ref)

    def k_tile(mp, kt):
        return mp.k_ref[0, 0, pl.ds(pl.multiple_of(kt * tk, tk), tk), :]

    def compute_scores(kt, buf):
        for mp, s_ref in zip(maps, buf):
            s_ref[...] = _dot_nt(k_tile(mp, kt), mp.q_ref[0, 0])

    def consume(kt, buf):
        for mp, s_ref in zip(maps, buf):
            _online_step(s_ref[...], mp.vt_ref[0, 0, kt], mp.m_ref, mp.acc_ref)

    @pl.when(i == 0)
    def _():
        compute_scores(0, buf_a)

    def pair(j):
        compute_scores(j + 1, buf_b)
        consume(j, buf_a)
        compute_scores(j + 2, buf_a)
        consume(j + 1, buf_b)

    odd = jnp.bitwise_and(i, 1)

    @pl.when(odd == 1)
    def _():
        pair(0)

    def body(jj, c):
        j = 2 * odd + 4 * jj
        pair(j)
        pair(j + 2)
        return c

    lax.fori_loop(0, lax.shift_right_logical(i, 1), body, 0)

    lower, upper = slice(0, tk), slice(tk, None)
    for mp, sb_ref in zip(maps, buf_b):
        sb_ref[:, lower] = _dot_nt(k_tile(mp, 2 * i + 1), mp.q_ref[0, 0, upper, :])
    for mp, sa_ref, sb_ref in zip(maps, buf_a, buf_b):
        bias = mp.bias_ref[...]
        vt0, vt1 = mp.vt_ref[0, 0, 2 * i], mp.vt_ref[0, 0, 2 * i + 1]
        _online_step(sa_ref[:, lower] + bias, vt0, mp.m_ref, mp.acc_ref, lanes=lower)
        s = jnp.concatenate([sa_ref[:, upper], sb_ref[:, lower] + bias], axis=0)
        _online_step(s, jnp.concatenate([vt0, vt1], axis=1), mp.m_ref, mp.acc_ref, lanes=upper)

    for mp, sa_ref in zip(maps, buf_a):
        sa_ref[...] = _dot_nt(k_tile(mp, 0), mp.q_next_ref[0, 0])


def _normalized(acc_ref, rows):
    acc = acc_ref[...]
    return acc[:rows] * (1.0 / acc[rows:rows + 1, :])


def _diag_bias(tk, chunked, slope2=None):
    key, qry = _tile_ids(tk)
    keep = (key // CHUNK) <= (qry // CHUNK) if chunked else key <= qry
    if slope2 is None:
        return jnp.where(keep, 0.0, MASKED)
    return jnp.where(keep, jnp.maximum(key - qry, 0).astype(F32) * -slope2, MASKED)


def _fox_mla_kernel(fq_ref, fqn_ref, fk_ref, fvt_ref, mq_ref, mqn_ref, mk_ref, mvt_ref, o_ref,
                    sfa_ref, sma_ref, sfb_ref, smb_ref, fbias_ref, mbias_ref,
                    fm_ref, facc_ref, mm_ref, macc_ref, *, tk):
    i = pl.program_id(2)

    @pl.when(i == 0)
    def _():
        fbias_ref[...] = _diag_bias(tk, False)
        mbias_ref[...] = _diag_bias(tk, True)

    maps = (_Map(fq_ref, fqn_ref, fk_ref, fvt_ref, fbias_ref, fm_ref, facc_ref),
            _Map(mq_ref, mqn_ref, mk_ref, mvt_ref, mbias_ref, mm_ref, macc_ref))
    _flash(i, maps, (sfa_ref, sma_ref), (sfb_ref, smb_ref), tk)

    o_t = jnp.concatenate([_normalized(mp.acc_ref, QK_DIM) for mp in maps], axis=0)
    o_ref[0] = o_t.T.astype(BF16)


def _diff_kernel(q1_ref, q1n_ref, q2_ref, q2n_ref, k1_ref, k2_ref, vt_ref, lam_ref, g_ref, o_ref,
                 s1a_ref, s2a_ref, s1b_ref, s2b_ref, bias_ref,
                 m1_ref, acc1_ref, m2_ref, acc2_ref, *, tk, lam_init):
    h = pl.program_id(1)
    i = pl.program_id(2)

    @pl.when(i == 0)
    def _():
        slope2 = jnp.exp2(jnp.full((1, 1), -2.0, F32) * (h + 1).astype(F32)) * (2.0 * LOG2E)
        bias_ref[...] = _diag_bias(tk, True, slope2)

    maps = (_Map(q1_ref, q1n_ref, k1_ref, vt_ref, bias_ref, m1_ref, acc1_ref),
            _Map(q2_ref, q2n_ref, k2_ref, vt_ref, bias_ref, m2_ref, acc2_ref))
    _flash(i, maps, (s1a_ref, s2a_ref), (s1b_ref, s2b_ref), tk)

    lams = lam_ref[...]
    lam = (jnp.exp(jnp.sum(lams[0:1] * lams[1:2], axis=-1, keepdims=True))
           - jnp.exp(jnp.sum(lams[2:3] * lams[3:4], axis=-1, keepdims=True)) + lam_init)
    o_t = _normalized(acc1_ref, HEAD_PAD) - lam * _normalized(acc2_ref, HEAD_PAD)
    o_ref[0] = (_rms(o_t.T, g_ref[...]) * (1.0 - lam_init)).astype(BF16)


def _attn_specs(s, v_rows):
    tq, tk = TILE_Q, TILE_K
    last = s // tq - 1
    q_spec = pl.BlockSpec((1, 1, tq, HEAD_PAD), lambda b, h, i: (b, h, i, 0))
    q_next_spec = pl.BlockSpec((1, 1, tq, HEAD_PAD),
                               lambda b, h, i: (b, h, jnp.minimum(i + 1, last), 0))
    k_spec = pl.BlockSpec((1, 1, s, HEAD_PAD), lambda b, h, i: (b, h, 0, 0))
    vt_spec = pl.BlockSpec((1, 1, s // tk, v_rows, tk), lambda b, h, i: (b, h, 0, 0, 0))
    o_spec = pl.BlockSpec((1, tq, HEAD_PAD), lambda b, h, i: (b, i, h))
    return (q_spec, q_next_spec), k_spec, vt_spec, o_spec


def _fox_mla_attention(fq, fk, fvt, mq, mk, mvt):
    b, _, s, _ = fq.shape
    tq, tk = TILE_Q, TILE_K
    q_spec, k_spec, vt_spec, o_spec = _attn_specs(s, V_ROWS_ONES)
    score = pltpu.VMEM((tk, tq), F32)
    bias = pltpu.VMEM((tk, tk), F32)
    stat = pltpu.VMEM((1, tq), F32)
    acc = pltpu.VMEM((V_ROWS_ONES, tq), F32)
    return pl.pallas_call(
        functools.partial(_fox_mla_kernel, tk=tk),
        grid=(b, HEADS, s // tq),
        in_specs=[*q_spec, k_spec, vt_spec, *q_spec, k_spec, vt_spec],
        out_specs=o_spec,
        out_shape=jax.ShapeDtypeStruct((b, s, _GROUP), BF16),
        scratch_shapes=[score, score, score, score, bias, bias, stat, acc, stat, acc],
        compiler_params=pltpu.CompilerParams(
            dimension_semantics=("arbitrary",) * 3, vmem_limit_bytes=VMEM_LIMIT),
        name="fox_mla_attn",
    )(fq, fq, fk, fvt, mq, mq, mk, mvt)


def _diff_attention(q1, q2, k1, k2, vt, lams, g, lam_init):
    b, _, s, _ = q1.shape
    tq, tk = TILE_Q, TILE_K
    q_spec, k_spec, vt_spec, o_spec = _attn_specs(s, V_ROWS_DIFF)
    score = pltpu.VMEM((tk, tq), F32)
    stat = pltpu.VMEM((1, tq), F32)
    acc = pltpu.VMEM((V_ROWS_DIFF, tq), F32)
    return pl.pallas_call(
        functools.partial(_diff_kernel, tk=tk, lam_init=lam_init),
        grid=(b, HEADS, s // tq),
        in_specs=[*q_spec, *q_spec, k_spec, k_spec, vt_spec,
                  pl.BlockSpec((4, QK_DIM), lambda b, h, i: (0, 0)),
                  pl.BlockSpec((1, HEAD_PAD), lambda b, h, i: (0, 0))],
        out_specs=o_spec,
        out_shape=jax.ShapeDtypeStruct((b, s, _GROUP), BF16),
        scratch_shapes=[score, score, score, score, pltpu.VMEM((tk, tk), F32),
                        stat, acc, stat, acc],
        compiler_params=pltpu.CompilerParams(
            dimension_semantics=("arbitrary",) * 3, vmem_limit_bytes=VMEM_LIMIT),
        name="diff_attn",
    )(q1, q1, q2, q2, k1, k2, vt, lams, g)


def _ffn_kernel(x_ref, ofm_ref, od_ref, wofm_ref, wod_ref, g2_ref,
                wup_ref, cw_ref, cb_ref, wdn_ref, gf_ref, o_ref, xn_ref, act_ref,
                *, tm, final):
    i = pl.program_id(1)

    @pl.when(i == 0)
    def _():
        xn_ref[0:HALO, :] = jnp.zeros((HALO, D_MODEL), BF16)

    @pl.when(i > 0)
    def _():
        xn_ref[0:HALO, :] = xn_ref[tm:tm + HALO, :]

    x1 = x_ref[0] + _dot(ofm_ref[0], wofm_ref[...]) + _dot(od_ref[0], wod_ref[...])
    xn_ref[HALO:HALO + tm, :] = _rms(x1, g2_ref[...]).astype(BF16)

    def conv(up, off):
        cw = cw_ref[:, off:off + FF_CHUNK]
        y = cb_ref[:, off:off + FF_CHUNK]
        for j in range(CONV_W):
            lo = HALO - (CONV_W - 1) + j
            y = y + up[lo:lo + tm, :] * cw[j:j + 1, :]
        return y

    xn = xn_ref[...]
    for c in range(D_FF // FF_CHUNK):
        og = c * FF_CHUNK
        ou = D_FF + og
        yg = conv(_dot(xn, wup_ref[:, og:og + FF_CHUNK]), og)
        yu = conv(_dot(xn, wup_ref[:, ou:ou + FF_CHUNK]), ou)
        act = yg * (1.0 / (1.0 + jnp.exp(-yg))) * yu
        act_ref[:, og:og + FF_CHUNK] = act.astype(BF16)

    x2 = x1 + _dot(act_ref[...], wdn_ref[...])
    if final:
        x2 = _rms(x2, gf_ref[...])
    o_ref[0] = x2


def _ffn(x, o_fm, o_diff, wo_fm, wo_d, g2, w_up, conv_w, conv_b, w_down, g_final, final):
    b, s, _ = x.shape
    tm = TILE_FFN
    row = lambda width: pl.BlockSpec((1, tm, width), lambda bi, i: (bi, i, 0))
    return pl.pallas_call(
        functools.partial(_ffn_kernel, tm=tm, final=final),
        grid=(b, s // tm),
        in_specs=[
            row(D_MODEL), row(_GROUP), row(_GROUP),
            _const_spec((_GROUP, D_MODEL)), _const_spec((_GROUP, D_MODEL)),
            _const_spec((1, D_MODEL)),
            _const_spec((D_MODEL, 2 * D_FF)),
            _const_spec((CONV_W, 2 * D_FF)),
            _const_spec((1, 2 * D_FF)),
            _const_spec((D_FF, D_MODEL)),
            _const_spec((1, D_MODEL)),
        ],
        out_specs=row(D_MODEL),
        out_shape=jax.ShapeDtypeStruct((b, s, D_MODEL), F32),
        scratch_shapes=[pltpu.VMEM((HALO + tm, D_MODEL), BF16), pltpu.VMEM((tm, D_FF), BF16)],
        compiler_params=pltpu.CompilerParams(
            dimension_semantics=("arbitrary", "arbitrary"), vmem_limit_bytes=VMEM_LIMIT),
        name="ffn",
    )(x, o_fm, o_diff, wo_fm, wo_d, g2, w_up, conv_w, conv_b, w_down, g_final)


def _rot_half_cols(w):
    half = MLA_ROPE // 2
    return jnp.concatenate([w[..., half:], w[..., :half]], axis=-1)


def _layout_w_in(w):
    n_ff = OFF_FF + HEADS
    n_kr = w.shape[-1] - MLA_ROPE
    zeros = lambda n: jnp.zeros(w.shape[:-1] + (n,), w.dtype)
    kr = w[..., n_kr:]
    rope_pad = HEAD_PAD - QK_DIM - MLA_ROPE
    out = jnp.concatenate(
        [w[..., :n_ff], zeros(128 - HEADS), w[..., n_ff:n_kr],
         zeros(QK_DIM), kr, zeros(rope_pad),
         zeros(QK_DIM), _rot_half_cols(kr), zeros(rope_pad)], axis=-1).astype(BF16)
    assert out.shape[-1] == PROJ_COLS
    return out


def _layout_w_uq(w):
    w = w.reshape(DEPTH, MLA_Q_RANK, HEADS, QK_DIM + MLA_ROPE)
    nope, rope = w[..., :QK_DIM], w[..., QK_DIM:]
    pad = jnp.zeros(w.shape[:-1] + (HEAD_PAD - QK_DIM - MLA_ROPE,), w.dtype)
    direct = jnp.concatenate([nope, rope, pad], axis=-1)
    swapped = jnp.concatenate([jnp.zeros_like(nope), _rot_half_cols(rope), pad], axis=-1)
    return jnp.concatenate([direct.reshape(DEPTH, MLA_Q_RANK, -1),
                            swapped.reshape(DEPTH, MLA_Q_RANK, -1)], axis=-1).astype(BF16)


def _layout_w_ukv(w):
    w = w.reshape(DEPTH, MLA_KV_RANK, HEADS, 2 * QK_DIM)
    pad = jnp.zeros(w.shape[:-1] + (HEAD_PAD - QK_DIM,), w.dtype)
    k = jnp.concatenate([w[..., :QK_DIM], pad], axis=-1).reshape(DEPTH, MLA_KV_RANK, -1)
    v = jnp.concatenate([w[..., QK_DIM:], pad], axis=-1).reshape(DEPTH, MLA_KV_RANK, -1)
    return jnp.concatenate([k, v], axis=-1).astype(BF16)


def _interleave_head_rows(w_fox, w_mla):
    parts = [w.reshape(DEPTH, HEADS, QK_DIM, -1) for w in (w_fox, w_mla)]
    return jnp.concatenate(parts, axis=2).reshape(DEPTH, HEADS * HEAD_PAD, -1)


def _rope_tables(s):
    half = MLA_ROPE // 2
    pos = jnp.arange(s, dtype=F32)
    inv = 1.0 / (ROPE_THETA ** (jnp.arange(half, dtype=F32) / half))
    ang = pos[:, None] * inv[None, :]
    cos, sin = jnp.cos(ang), jnp.sin(ang)
    tail = jnp.zeros((s, HEAD_PAD - QK_DIM - MLA_ROPE), F32)
    cos_t = jnp.concatenate([jnp.ones((s, QK_DIM), F32), cos, cos, tail], axis=1)
    sin_t = jnp.concatenate([jnp.zeros((s, QK_DIM), F32), -sin, sin, tail], axis=1)
    return cos_t, sin_t


def kernel(x, ln1_g, w_in, fgate_b, lam_q1, lam_k1, lam_q2, lam_k2, diff_norm_g, q_norm_g, w_uq,
           kv_norm_g, w_ukv, w_o, ln2_g, w_up, conv_w, conv_b, w_down, final_g):
    s = x.shape[1]
    cos_t, sin_t = _rope_tables(s)
    row = lambda v: v.reshape(DEPTH, 1, -1)
    wa = _layout_w_in(w_in)
    fb = row(jnp.pad(fgate_b, ((0, 0), (0, 128 - HEADS))))
    wuq = _layout_w_uq(w_uq)
    wukv = _layout_w_ukv(w_ukv)
    lams = jnp.stack([lam_q1, lam_k1, lam_q2, lam_k2], axis=1)
    wo = w_o.astype(BF16)
    wo_fm = _interleave_head_rows(wo[:, :_FOX], wo[:, _FOX + _GROUP:])
    wo_d = wo[:, _FOX:_FOX + _GROUP]
    wup = w_up.astype(BF16)
    wdn = w_down.astype(BF16)
    g1, g2, qng, kvng, dng, cb = (row(v) for v in (ln1_g, ln2_g, q_norm_g, kv_norm_g,
                                                   diff_norm_g, conv_b))
    for i in range(DEPTH):
        lam_init = 0.8 - 0.6 * math.exp(-0.3 * i)
        heads = _project(x, g1[i], wa[i], fb[i], qng[i], wuq[i], kvng[i], wukv[i], cos_t, sin_t)
        fq, fk, fvt, dq1, dq2, dk1, dk2, dvt, mq, mk, mvt = heads
        o_fm = _fox_mla_attention(fq, fk, fvt, mq, mk, mvt)
        o_diff = _diff_attention(dq1, dq2, dk1, dk2, dvt, lams[i], dng[i], lam_init)
        x = _ffn(x, o_fm, o_diff, wo_fm[i], wo_d[i], g2[i], wup[i], conv_w[i],
                 cb[i], wdn[i], final_g.reshape(1, -1), final=(i == DEPTH - 1))
    return x
```

```python
import functools
import math
from typing import NamedTuple

import jax
import jax.numpy as jnp
from jax import lax
from jax.experimental import pallas as pl
from jax.experimental.pallas import tpu as pltpu

F32 = jnp.float32
BF16 = jnp.bfloat16

D_MODEL = 1024
DEPTH = 2
CHUNK = 64
EPS = 1e-6
MASKED = -1e30

HEADS = 4
HEAD_PAD = 128
QK_DIM = 64
MLA_ROPE = 32
MLA_Q_RANK = 256
MLA_KV_RANK = 128
ROPE_THETA = 10000.0
D_FF = 2816
CONV_W = 3

LOG2E = 1.4426950408889634

_GROUP = HEADS * HEAD_PAD
_FOX = HEADS * QK_DIM
OFF_FQ, OFF_FK, OFF_FV = 0, _FOX, 2 * _FOX
OFF_FF = 3 * _FOX
OFF_DQ = OFF_FF + 128
OFF_DK = OFF_DQ + _GROUP
OFF_DV = OFF_DK + _GROUP
OFF_CQ = OFF_DV + _GROUP
OFF_CKV = OFF_CQ + MLA_Q_RANK
OFF_KR = OFF_CKV + MLA_KV_RANK
OFF_KRP = OFF_KR + 128
PROJ_COLS = OFF_KRP + 128

TILE_K = 512
TILE_Q = 2 * TILE_K
TILE_PROJ = TILE_K
BF16_ROWS = 16
V_ROWS_ONES = QK_DIM + BF16_ROWS
V_ROWS_DIFF = HEAD_PAD + BF16_ROWS
TILE_FFN = 512
FF_CHUNK = 256
HALO = 16
VMEM_LIMIT = 56 * 1024 * 1024


def _dot(a, b):
    return jnp.dot(a, b, preferred_element_type=F32)


def _dot_nt(a, b):
    return lax.dot_general(a, b, (((1,), (1,)), ((), ())), preferred_element_type=F32)


def _rms(x, g):
    return x * lax.rsqrt(jnp.mean(x * x, axis=-1, keepdims=True) + EPS) * g


def _split3(x):
    hi = x.astype(BF16)
    r1 = x - hi.astype(F32)
    mid = r1.astype(BF16)
    lo = (r1 - mid.astype(F32)).astype(BF16)
    return hi, mid, lo


def _const_spec(shape):
    nd = len(shape)
    return pl.BlockSpec(shape, lambda *_: (0,) * nd, pipeline_mode=pl.Buffered(1))


def _layer_spec(shape, layer):
    nd = len(shape)
    return pl.BlockSpec((None,) + tuple(shape), lambda *_: (layer,) + (0,) * nd,
                        pipeline_mode=pl.Buffered(1))


def _proj_kernel(x_ref, g1_ref, wa_ref, fb_ref, qng_ref, wuq_ref, kvng_ref, wukv_ref,
                 cos_ref, sin_ref,
                 fq_ref, fk_ref, fvt_ref, dq1_ref, dq2_ref, dk1_ref, dk2_ref, dvt_ref,
                 mq_ref, mk_ref, mvt_ref, carry_ref, *, tm):
    i = pl.program_id(1)

    @pl.when(i == 0)
    def _():
        carry_ref[...] = jnp.zeros_like(carry_ref)

    xn = _rms(x_ref[0], g1_ref[...]).astype(BF16)

    def proj(off, width):
        return _dot(xn, wa_ref[:, off:off + width])

    def head(t, h):
        return t[:, h * HEAD_PAD:(h + 1) * HEAD_PAD]

    lane = lax.broadcasted_iota(jnp.int32, (tm, HEAD_PAD), 1)
    low = lane < QK_DIM
    extra = jnp.where(low, lane, lane - QK_DIM)
    in_half = (low, jnp.logical_not(low))
    ones3 = tuple(jnp.where(jnp.logical_not(m) & (extra < 3), 1.0, 0.0) for m in in_half)
    one1 = tuple(jnp.where(jnp.logical_not(m) & (extra == 0), 1.0, 0.0) for m in in_half)
    qk_scale = QK_DIM ** -0.5 * LOG2E

    def v_rows(t, half):
        t = t.T
        if half == 0:
            return t[:V_ROWS_ONES]
        return jnp.concatenate([t[QK_DIM:], t[:V_ROWS_ONES - QK_DIM]], axis=0)

    z = proj(OFF_FF, 128) + fb_ref[...]
    logf = jnp.minimum(z, 0.0) - jnp.log1p(jnp.exp(-jnp.abs(z)))
    rr = lax.broadcasted_iota(jnp.int32, (tm, tm), 0)
    cc = lax.broadcasted_iota(jnp.int32, (tm, tm), 1)
    tri = (cc <= rr).astype(BF16)
    parts = _dot(tri, jnp.concatenate(_split3(logf), axis=1))
    cum = parts[:, :128] + parts[:, 128:256] + parts[:, 256:] + carry_ref[...]
    carry_ref[...] = cum[tm - 1:tm, :]
    pr = lax.broadcasted_iota(jnp.int32, (3 * 128, _GROUP), 0)
    pc = lax.broadcasted_iota(jnp.int32, (3 * 128, _GROUP), 1)
    term, hd = pr // 128, pr % 128
    place = pc - hd * HEAD_PAD - jnp.where(hd % 2 == 0, QK_DIM, 0) == term
    fk_bias = _dot(jnp.concatenate(_split3(cum * (-LOG2E)), axis=1), place.astype(BF16))
    hq = proj(OFF_FQ, _FOX)
    hk = proj(OFF_FK, _FOX)
    hv = proj(OFF_FV, _FOX)
    for h in range(HEADS):
        pair, half = divmod(h, 2)
        keep = in_half[half]
        fq_ref[0, h] = jnp.where(keep, head(hq, pair) * qk_scale, ones3[half]).astype(BF16)
        fk_ref[0, h] = jnp.where(keep, head(hk, pair), head(fk_bias, h)).astype(BF16)
        fvt_ref[0, h, 0] = v_rows(jnp.where(keep, head(hv, pair), one1[half]), half).astype(BF16)

    pos = (i * tm + lax.broadcasted_iota(jnp.int32, (tm, HEAD_PAD), 0)).astype(F32)
    hq = proj(OFF_DQ, _GROUP)
    hk = proj(OFF_DK, _GROUP)
    hv = proj(OFF_DV, _GROUP)
    ones_rows = jnp.where(lax.broadcasted_iota(jnp.int32, (BF16_ROWS, tm), 0) == 0, 1.0, 0.0)
    for h in range(HEADS):
        slope =2.0 ** (-8.0 * (h + 1) / HEADS)
        a_hi, a_mid, a_lo = _split3(pos * (slope * LOG2E))
        kbias = jnp.where(extra == 0, a_hi.astype(F32),
                          jnp.where(extra == 1, a_mid.astype(F32),
                                    jnp.where(extra == 2, a_lo.astype(F32), 0.0)))
        q = head(hq, h) * qk_scale
        k = head(hk, h)
        for half, (q_ref, k_ref) in enumerate(((dq1_ref, dk1_ref), (dq2_ref, dk2_ref))):
            q_ref[0, h] = jnp.where(in_half[half], q, ones3[half]).astype(BF16)
            k_ref[0, h] = jnp.where(in_half[half], k, kbias).astype(BF16)
        dvt_ref[0, h, 0] = jnp.concatenate([head(hv, h).T, ones_rows], axis=0).astype(BF16)

    cqn = _rms(proj(OFF_CQ, MLA_Q_RANK), qng_ref[...]).astype(BF16)
    ckvn = _rms(proj(OFF_CKV, MLA_KV_RANK), kvng_ref[...]).astype(BF16)
    cos = cos_ref[...]
    sin = sin_ref[...]
    krot = proj(OFF_KR, 128) * cos + proj(OFF_KRP, 128) * sin
    qa = _dot(cqn, wuq_ref[:, :_GROUP])
    qb = _dot(cqn, wuq_ref[:, _GROUP:])
    kk = _dot(ckvn, wukv_ref[:, :_GROUP])
    vv = _dot(ckvn, wukv_ref[:, _GROUP:])
    mla_scale = (QK_DIM + MLA_ROPE) ** -0.5 * LOG2E
    for h in range(HEADS):
        mq_ref[0, h] = ((head(qa, h) * cos + head(qb, h) * sin) * mla_scale).astype(BF16)
        mk_ref[0, h] = (head(kk, h) + krot).astype(BF16)
        mvt_ref[0, h, 0] = v_rows(head(vv, h) + one1[0], 0).astype(BF16)


def _project(x, layer, g1, wa, fb, qng, wuq, kvng, wukv, cos_t, sin_t):
    b, s, _ = x.shape
    per_layer = functools.partial(_layer_spec, layer=layer)
    tm = TILE_PROJ
    rows = (jax.ShapeDtypeStruct((b, HEADS, s, HEAD_PAD), BF16),
            pl.BlockSpec((1, HEADS, tm, HEAD_PAD), lambda bi, i: (bi, 0, i, 0)))
    def cols(v_rows):
        return (jax.ShapeDtypeStruct((b, HEADS, s // tm, v_rows, tm), BF16),
                pl.BlockSpec((1, HEADS, 1, v_rows, tm), lambda bi, i: (bi, 0, i, 0, 0)))

    outs = [rows, rows, cols(V_ROWS_ONES), rows, rows, rows, rows, cols(V_ROWS_DIFF),
            rows, rows, cols(V_ROWS_ONES)]
    row_tab = pl.BlockSpec((tm, HEAD_PAD), lambda bi, i: (i, 0))
    return pl.pallas_call(
        functools.partial(_proj_kernel, tm=tm),
        grid=(b, s // tm),
        in_specs=[
            pl.BlockSpec((1, tm, D_MODEL), lambda bi, i: (bi, i, 0)),
            per_layer((1, D_MODEL)),
            per_layer((D_MODEL, PROJ_COLS)),
            per_layer((1, 128)),
            per_layer((1, MLA_Q_RANK)),
            per_layer((MLA_Q_RANK, 2 * _GROUP)),
            per_layer((1, MLA_KV_RANK)),
            per_layer((MLA_KV_RANK, 2 * _GROUP)),
            row_tab, row_tab,
        ],
        out_specs=[o[1] for o in outs],
        out_shape=[o[0] for o in outs],
        scratch_shapes=[pltpu.VMEM((1, 128), F32)],
        compiler_params=pltpu.CompilerParams(
            dimension_semantics=("arbitrary", "arbitrary"), vmem_limit_bytes=VMEM_LIMIT),
        name="proj",
    )(x, g1, wa, fb, qng, wuq, kvng, wukv, cos_t, sin_t)


def _tile_ids(t):
    key = lax.broadcasted_iota(jnp.int32, (t, t), 0)
    qry = lax.broadcasted_iota(jnp.int32, (t, t), 1)
    return key, qry


def _online_step(s, vt, m_ref, acc_ref, lanes=slice(None)):
    m_prev = m_ref[:, lanes]
    m_new = jnp.maximum(m_prev, jnp.max(s, axis=0, keepdims=True))
    p = jnp.exp2(s - m_new)
    alpha = jnp.exp2(m_prev - m_new)
    acc_ref[:, lanes] = alpha * acc_ref[:, lanes] + _dot(vt, p.astype(BF16))
    m_ref[:, lanes] = m_new


class _Map(NamedTuple):
    q_ref: object
    q_next_ref: object
    k_ref: object
    vt_ref: object
    bias_ref: object
    m_ref: object
    acc_ref: object


def _flash(i, maps, buf_a, buf_b, tk):
    for mp in maps:
        mp.m_ref[...] = jnp.full_like(mp.m_ref, MASKED)
        mp.acc_ref[...] = jnp.zeros_like(mp.acc_ref)

    def k_tile(mp, kt):
        return mp.k_ref[0, 0, pl.ds(pl.multiple_of(kt * tk, tk), tk), :]

    def compute_scores(kt, buf):
        for mp, s_ref in zip(maps, buf):
            s_ref[...] = _dot_nt(k_tile(mp, kt), mp.q_ref[0, 0])

    def consume(kt, buf):
        for mp, s_ref in zip(maps, buf):
            _online_step(s_ref[...], mp.vt_ref[0, 0, kt], mp.m_ref, mp.acc_ref)

    @pl.when(i == 0)
    def _():
        compute_scores(0, buf_a)

    def pair(j):
        compute_scores(j + 1, buf_b)
        consume(j, buf_a)
        compute_scores(j + 2, buf_a)
        consume(j + 1, buf_b)

    odd = jnp.bitwise_and(i, 1)

    @pl.when(odd == 1)
    def _():
        pair(0)

    def body(jj, c):
        j = 2 * odd + 4 * jj
        pair(j)
        pair(j + 2)
        return c

    lax.fori_loop(0, lax.shift_right_logical(i, 1), body, 0)

    lower, upper = slice(0, tk), slice(tk, None)
    for mp, sb_ref in zip(maps, buf_b):
        sb_ref[:, lower] = _dot_nt(k_tile(mp, 2 * i + 1), mp.q_ref[0, 0, upper, :])
    for mp, sa_ref, sb_ref in zip(maps, buf_a, buf_b):
        bias = mp.bias_ref[...]
        vt0, vt1 = mp.vt_ref[0, 0, 2 * i], mp.vt_ref[0, 0, 2 * i + 1]
        _online_step(sa_ref[:, lower] + bias, vt0, mp.m_ref, mp.acc_ref, lanes=lower)
        s = jnp.concatenate([sa_ref[:, upper], sb_ref[:, lower] + bias], axis=0)
        _online_step(s, jnp.concatenate([vt0, vt1], axis=1), mp.m_ref, mp.acc_ref, lanes=upper)

    for mp, sa_ref in zip(maps, buf_a):
        sa_ref[...] = _dot_nt(k_tile(mp, 0), mp.q_next_ref[0, 0])


def _normalized(acc_ref, rows):
    acc = acc_ref[...]
    return acc[:rows] * (1.0 / acc[rows:rows + 1, :])


def _diag_bias(tk, chunked, slope2=None):
    key, qry = _tile_ids(tk)
    keep = (key // CHUNK) <= (qry // CHUNK) if chunked else key <= qry
    if slope2 is None:
        return jnp.where(keep, 0.0, MASKED)
    return jnp.where(keep, jnp.maximum(key - qry, 0).astype(F32) * -slope2, MASKED)


def _fox_mla_kernel(fq_ref, fqn_ref, fk_ref, fvt_ref, mq_ref, mqn_ref, mk_ref, mvt_ref, o_ref,
                    sfa_ref, sma_ref, sfb_ref, smb_ref, fbias_ref, mbias_ref,
                    fm_ref, facc_ref, mm_ref, macc_ref, *, tk):
    i = pl.program_id(2)

    @pl.when(i == 0)
    def _():
        fbias_ref[...] = _diag_bias(tk, False)
        mbias_ref[...] = _diag_bias(tk, True)

    maps = (_Map(fq_ref, fqn_ref, fk_ref, fvt_ref, fbias_ref, fm_ref, facc_ref),
            _Map(mq_ref, mqn_ref, mk_ref, mvt_ref, mbias_ref, mm_ref, macc_ref))
    _flash(i, maps, (sfa_ref, sma_ref), (sfb_ref, smb_ref), tk)

    o_t = jnp.concatenate([_normalized(mp.acc_ref, QK_DIM) for mp in maps], axis=0)
    o_ref[0] = o_t.T.astype(BF16)


def _diff_kernel(q1_ref, q1n_ref, q2_ref, q2n_ref, k1_ref, k2_ref, vt_ref, lam_ref, g_ref, o_ref,
                 s1a_ref, s2a_ref, s1b_ref, s2b_ref, bias_ref,
                 m1_ref, acc1_ref, m2_ref, acc2_ref, *, tk, lam_init):
    h = pl.program_id(1)
    i = pl.program_id(2)

    @pl.when(i == 0)
    def _():
        slope2 = jnp.exp2(jnp.full((1, 1), -2.0, F32) * (h + 1).astype(F32)) * (2.0 * LOG2E)
        bias_ref[...] = _diag_bias(tk, True, slope2)

    maps = (_Map(q1_ref, q1n_ref, k1_ref, vt_ref, bias_ref, m1_ref, acc1_ref),
            _Map(q2_ref, q2n_ref, k2_ref, vt_ref, bias_ref, m2_ref, acc2_ref))
    _flash(i, maps, (s1a_ref, s2a_ref), (s1b_ref, s2b_ref), tk)

    lams = lam_ref[...]
    lam = (jnp.exp(jnp.sum(lams[0:1] * lams[1:2], axis=-1, keepdims=True))
           - jnp.exp(jnp.sum(lams[2:3] * lams[3:4], axis=-1, keepdims=True)) + lam_init)
    o_t = _normalized(acc1_ref, HEAD_PAD) - lam * _normalized(acc2_ref, HEAD_PAD)
    o_ref[0] = (_rms(o_t.T, g_ref[...]) * (1.0 - lam_init)).astype(BF16)


def _attn_specs(s, v_rows):
    tq, tk = TILE_Q, TILE_K
    last = s // tq - 1
    q_spec = pl.BlockSpec((1, 1, tq, HEAD_PAD), lambda b, h, i: (b, h, i, 0))
    q_next_spec = pl.BlockSpec((1, 1, tq, HEAD_PAD),
                               lambda b, h, i: (b, h, jnp.minimum(i + 1, last), 0))
    k_spec = pl.BlockSpec((1, 1, s, HEAD_PAD), lambda b, h, i: (b, h, 0, 0))
    vt_spec = pl.BlockSpec((1, 1, s // tk, v_rows, tk), lambda b, h, i: (b, h, 0, 0, 0))
    o_spec = pl.BlockSpec((1, tq, HEAD_PAD), lambda b, h, i: (b, i, h))
    return (q_spec, q_next_spec), k_spec, vt_spec, o_spec


def _fox_mla_attention(fq, fk, fvt, mq, mk, mvt):
    b, _, s, _ = fq.shape
    tq, tk = TILE_Q, TILE_K
    q_spec, k_spec, vt_spec, o_spec = _attn_specs(s, V_ROWS_ONES)
    score = pltpu.VMEM((tk, tq), F32)
    bias = pltpu.VMEM((tk, tk), F32)
    stat = pltpu.VMEM((1, tq), F32)
    acc = pltpu.VMEM((V_ROWS_ONES, tq), F32)
    return pl.pallas_call(
        functools.partial(_fox_mla_kernel, tk=tk),
        grid=(b, HEADS, s // tq),
        in_specs=[*q_spec, k_spec, vt_spec, *q_spec, k_spec, vt_spec],
        out_specs=o_spec,
        out_shape=jax.ShapeDtypeStruct((b, s, _GROUP), BF16),
        scratch_shapes=[score, score, score, score, bias, bias, stat, acc, stat, acc],
        compiler_params=pltpu.CompilerParams(
            dimension_semantics=("arbitrary",) * 3, vmem_limit_bytes=VMEM_LIMIT),
        name="fox_mla_attn",
    )(fq, fq, fk, fvt, mq, mq, mk, mvt)


def _diff_attention(q1, q2, k1, k2, vt, layer, lams, g, lam_init):
    b, _, s, _ = q1.shape
    tq, tk = TILE_Q, TILE_K
    q_spec, k_spec, vt_spec, o_spec = _attn_specs(s, V_ROWS_DIFF)
    score = pltpu.VMEM((tk, tq), F32)
    stat = pltpu.VMEM((1, tq), F32)
    acc = pltpu.VMEM((V_ROWS_DIFF, tq), F32)
    return pl.pallas_call(
        functools.partial(_diff_kernel, tk=tk, lam_init=lam_init),
        grid=(b, HEADS, s // tq),
        in_specs=[*q_spec, *q_spec, k_spec, k_spec, vt_spec,
                  _layer_spec((4, QK_DIM), layer), _layer_spec((1, HEAD_PAD), layer)],
        out_specs=o_spec,
        out_shape=jax.ShapeDtypeStruct((b, s, _GROUP), BF16),
        scratch_shapes=[score, score, score, score, pltpu.VMEM((tk, tk), F32),
                        stat, acc, stat, acc],
        compiler_params=pltpu.CompilerParams(
            dimension_semantics=("arbitrary",) * 3, vmem_limit_bytes=VMEM_LIMIT),
        name="diff_attn",
    )(q1, q1, q2, q2, k1, k2, vt, lams, g)


def _ffn_kernel(x_ref, ofm_ref, od_ref, wofm_ref, wod_ref, g2_ref,
                wup_ref, cw_ref, cb_ref, wdn_ref, gf_ref, o_ref, xn_ref, act_ref,
                *, tm, final):
    i = pl.program_id(1)

    @pl.when(i == 0)
    def _():
        xn_ref[0:HALO, :] = jnp.zeros((HALO, D_MODEL), BF16)

    @pl.when(i > 0)
    def _():
        xn_ref[0:HALO, :] = xn_ref[tm:tm + HALO, :]

    x1 = x_ref[0] + _dot(ofm_ref[0], wofm_ref[...]) + _dot(od_ref[0], wod_ref[...])
    xn_ref[HALO:HALO + tm, :] = _rms(x1, g2_ref[...]).astype(BF16)

    def conv(up, off):
        cw = cw_ref[:, off:off + FF_CHUNK]
        y = cb_ref[:, off:off + FF_CHUNK]
        for j in range(CONV_W):
            lo = HALO - (CONV_W - 1) + j
            y = y + up[lo:lo + tm, :] * cw[j:j + 1, :]
        return y

    xn = xn_ref[...]
    for c in range(D_FF // FF_CHUNK):
        og = c * FF_CHUNK
        ou = D_FF + og
        yg = conv(_dot(xn, wup_ref[:, og:og + FF_CHUNK]), og)
        yu = conv(_dot(xn, wup_ref[:, ou:ou + FF_CHUNK]), ou)
        act = yg * (1.0 / (1.0 + jnp.exp(-yg))) * yu
        act_ref[:, og:og + FF_CHUNK] = act.astype(BF16)

    x2 = x1 + _dot(act_ref[...], wdn_ref[...])
    if final:
        x2 = _rms(x2, gf_ref[...])
    o_ref[0] = x2


def _ffn(x, o_fm, o_diff, layer, wo_fm, wo_d, g2, w_up, conv_w, conv_b, w_down, g_final, final):
    b, s, _ = x.shape
    per_layer = functools.partial(_layer_spec, layer=layer)
    tm = TILE_FFN
    row = lambda width: pl.BlockSpec((1, tm, width), lambda bi, i: (bi, i, 0))
    return pl.pallas_call(
        functools.partial(_ffn_kernel, tm=tm, final=final),
        grid=(b, s // tm),
        in_specs=[
            row(D_MODEL), row(_GROUP), row(_GROUP),
            per_layer((_GROUP, D_MODEL)), per_layer((_GROUP, D_MODEL)),
            per_layer((1, D_MODEL)),
            per_layer((D_MODEL, 2 * D_FF)),
            per_layer((CONV_W, 2 * D_FF)),
            per_layer((1, 2 * D_FF)),
            per_layer((D_FF, D_MODEL)),
            _const_spec((1, D_MODEL)),
        ],
        out_specs=row(D_MODEL),
        out_shape=jax.ShapeDtypeStruct((b, s, D_MODEL), F32),
        scratch_shapes=[pltpu.VMEM((HALO + tm, D_MODEL), BF16), pltpu.VMEM((tm, D_FF), BF16)],
        compiler_params=pltpu.CompilerParams(
            dimension_semantics=("arbitrary", "arbitrary"), vmem_limit_bytes=VMEM_LIMIT),
        name="ffn",
    )(x, o_fm, o_diff, wo_fm, wo_d, g2, w_up, conv_w, conv_b, w_down, g_final)


def _rot_half_cols(w):
    half = MLA_ROPE // 2
    return jnp.concatenate([w[..., half:], w[..., :half]], axis=-1)


def _layout_w_in(w):
    n_ff = OFF_FF + HEADS
    n_kr = w.shape[-1] - MLA_ROPE
    zeros = lambda n: jnp.zeros(w.shape[:-1] + (n,), w.dtype)
    kr = w[..., n_kr:]
    rope_pad = HEAD_PAD - QK_DIM - MLA_ROPE
    out = jnp.concatenate(
        [w[..., :n_ff], zeros(128 - HEADS), w[..., n_ff:n_kr],
         zeros(QK_DIM), kr, zeros(rope_pad),
         zeros(QK_DIM), _rot_half_cols(kr), zeros(rope_pad)], axis=-1).astype(BF16)
    assert out.shape[-1] == PROJ_COLS
    return out


def _layout_w_uq(w):
    w = w.reshape(DEPTH, MLA_Q_RANK, HEADS, QK_DIM + MLA_ROPE)
    nope, rope = w[..., :QK_DIM], w[..., QK_DIM:]
    pad = jnp.zeros(w.shape[:-1] + (HEAD_PAD - QK_DIM - MLA_ROPE,), w.dtype)
    direct = jnp.concatenate([nope, rope, pad], axis=-1)
    swapped = jnp.concatenate([jnp.zeros_like(nope), _rot_half_cols(rope), pad], axis=-1)
    return jnp.concatenate([direct.reshape(DEPTH, MLA_Q_RANK, -1),
                            swapped.reshape(DEPTH, MLA_Q_RANK, -1)], axis=-1).astype(BF16)


def _layout_w_ukv(w):
    w = w.reshape(DEPTH, MLA_KV_RANK, HEADS, 2 * QK_DIM)
    pad = jnp.zeros(w.shape[:-1] + (HEAD_PAD - QK_DIM,), w.dtype)
    k = jnp.concatenate([w[..., :QK_DIM], pad], axis=-1).reshape(DEPTH, MLA_KV_RANK, -1)
    v = jnp.concatenate([w[..., QK_DIM:], pad], axis=-1).reshape(DEPTH, MLA_KV_RANK, -1)
    return jnp.concatenate([k, v], axis=-1).astype(BF16)


def _interleave_head_rows(w_fox, w_mla):
    parts = [w.reshape(DEPTH, HEADS, QK_DIM, -1) for w in (w_fox, w_mla)]
    return jnp.concatenate(parts, axis=2).reshape(DEPTH, HEADS * HEAD_PAD, -1)


def _rope_tables(s):
    half = MLA_ROPE // 2
    inv = 1.0 / (ROPE_THETA ** (jnp.arange(half, dtype=F32) / half))
    pos = (jnp.arange(s * half, dtype=jnp.int32) // half).astype(F32)
    ang = pos * jnp.tile(inv, s)
    cos, sin = jnp.cos(ang).reshape(s, half), jnp.sin(ang).reshape(s, half)
    tail = jnp.zeros((s, HEAD_PAD - QK_DIM - MLA_ROPE), F32)
    cos_t = jnp.concatenate([jnp.ones((s, QK_DIM), F32), cos, cos, tail], axis=1)
    sin_t = jnp.concatenate([jnp.zeros((s, QK_DIM), F32), -sin, sin, tail], axis=1)
    return cos_t, sin_t


def kernel(x, ln1_g, w_in, fgate_b, lam_q1, lam_k1, lam_q2, lam_k2, diff_norm_g, q_norm_g, w_uq,
           kv_norm_g, w_ukv, w_o, ln2_g, w_up, conv_w, conv_b, w_down, final_g):
    s = x.shape[1]
    cos_t, sin_t = _rope_tables(s)
    row = lambda v: v.reshape(DEPTH, 1, -1)
    wa = _layout_w_in(w_in)
    fb = row(jnp.pad(fgate_b, ((0, 0), (0, 128 - HEADS))))
    wuq = _layout_w_uq(w_uq)
    wukv = _layout_w_ukv(w_ukv)
    lams = jnp.stack([lam_q1, lam_k1, lam_q2, lam_k2], axis=1)
    wo = w_o.astype(BF16)
    wo_fm = _interleave_head_rows(wo[:, :_FOX], wo[:, _FOX + _GROUP:])
    wo_d = wo[:, _FOX:_FOX + _GROUP]
    wup = w_up.astype(BF16)
    wdn = w_down.astype(BF16)
    g1, g2, qng, kvng, dng, cb = (row(v) for v in (ln1_g, ln2_g, q_norm_g, kv_norm_g,
                                                   diff_norm_g, conv_b))
    for i in range(DEPTH):
        lam_init = 0.8 - 0.6 * math.exp(-0.3 * i)
        heads = _project(x, i, g1, wa, fb, qng, wuq, kvng, wukv, cos_t, sin_t)
        fq, fk, fvt, dq1, dq2, dk1, dk2, dvt, mq, mk, mvt = heads
        o_fm = _fox_mla_attention(fq, fk, fvt, mq, mk, mvt)
        o_diff = _diff_attention(dq1, dq2, dk1, dk2, dvt, i, lams, dng, lam_init)
        x = _ffn(x, o_fm, o_diff, i, wo_fm, wo_d, g2, wup, conv_w, cb, wdn,
                 final_g.reshape(1, -1), final=(i == DEPTH - 1))
    return x
```

```python
import functools
import math
from typing import NamedTuple

import jax
import jax.numpy as jnp
from jax import lax
from jax.experimental import pallas as pl
from jax.experimental.pallas import tpu as pltpu

F32 = jnp.float32
BF16 = jnp.bfloat16

D_MODEL = 1024
DEPTH = 2
CHUNK = 64
EPS = 1e-6
MASKED = -1e30

HEADS = 4
HEAD_PAD = 128
QK_DIM = 64
MLA_ROPE = 32
MLA_Q_RANK = 256
MLA_KV_RANK = 128
ROPE_THETA = 10000.0
D_FF = 2816
CONV_W = 3

LOG2E = 1.4426950408889634

_GROUP = HEADS * HEAD_PAD
_FOX = HEADS * QK_DIM
OFF_FQ, OFF_FK, OFF_FV = 0, _FOX, 2 * _FOX
OFF_FF = 3 * _FOX
OFF_DQ = OFF_FF + 128
OFF_DK = OFF_DQ + _GROUP
OFF_DV = OFF_DK + _GROUP
OFF_CQ = OFF_DV + _GROUP
OFF_CKV = OFF_CQ + MLA_Q_RANK
OFF_KR = OFF_CKV + MLA_KV_RANK
OFF_KRP = OFF_KR + 128
PROJ_COLS = OFF_KRP + 128

TILE_K = 512
TILE_Q = 2 * TILE_K
TILE_PROJ = TILE_K
BF16_ROWS = 16
V_ROWS_ONES = QK_DIM + BF16_ROWS
V_ROWS_DIFF = HEAD_PAD + BF16_ROWS
TILE_FFN = 512
FF_CHUNK = 256
HALO = 16
VMEM_LIMIT = 56 * 1024 * 1024


def _dot(a, b):
    return jnp.dot(a, b, preferred_element_type=F32)


def _dot_nt(a, b):
    return lax.dot_general(a, b, (((1,), (1,)), ((), ())), preferred_element_type=F32)


def _rms(x, g):
    return x * lax.rsqrt(jnp.mean(x * x, axis=-1, keepdims=True) + EPS) * g


def _split3(x):
    hi = x.astype(BF16)
    r1 = x - hi.astype(F32)
    mid = r1.astype(BF16)
    lo = (r1 - mid.astype(F32)).astype(BF16)
    return hi, mid, lo


def _const_spec(shape):
    nd = len(shape)
    return pl.BlockSpec(shape, lambda *_: (0,) * nd, pipeline_mode=pl.Buffered(1))


def _layer_spec(shape, layer):
    nd = len(shape)
    return pl.BlockSpec((None,) + tuple(shape), lambda *_: (layer,) + (0,) * nd,
                        pipeline_mode=pl.Buffered(1))


def _proj_kernel(x_ref, g1_ref, wa_ref, fb_ref, qng_ref, wuq_ref, kvng_ref, wukv_ref,
                 cos_ref, sin_ref,
                 fq_ref, fk_ref, fvt_ref, dq1_ref, dq2_ref, dk1_ref, dk2_ref, dvt_ref,
                 mq_ref, mk_ref, mvt_ref, carry_ref, *, tm):
    i = pl.program_id(1)

    @pl.when(i == 0)
    def _():
        carry_ref[...] = jnp.zeros_like(carry_ref)

    xn = _rms(x_ref[0], g1_ref[...]).astype(BF16)

    def proj(off, width):
        return _dot(xn, wa_ref[:, off:off + width])

    def head(t, h):
        return t[:, h * HEAD_PAD:(h + 1) * HEAD_PAD]

    lane = lax.broadcasted_iota(jnp.int32, (tm, HEAD_PAD), 1)
    low = lane < QK_DIM
    extra = jnp.where(low, lane, lane - QK_DIM)
    in_half = (low, jnp.logical_not(low))
    ones3 = tuple(jnp.where(jnp.logical_not(m) & (extra < 3), 1.0, 0.0) for m in in_half)
    one1 = tuple(jnp.where(jnp.logical_not(m) & (extra == 0), 1.0, 0.0) for m in in_half)
    qk_scale = QK_DIM ** -0.5 * LOG2E

    def v_rows(t, half):
        t = t.T
        if half == 0:
            return t[:V_ROWS_ONES]
        return jnp.concatenate([t[QK_DIM:], t[:V_ROWS_ONES - QK_DIM]], axis=0)

    z = proj(OFF_FF, 128) + fb_ref[...]
    logf = jnp.minimum(z, 0.0) - jnp.log1p(jnp.exp(-jnp.abs(z)))
    rr = lax.broadcasted_iota(jnp.int32, (tm, tm), 0)
    cc = lax.broadcasted_iota(jnp.int32, (tm, tm), 1)
    tri = (cc <= rr).astype(BF16)
    parts = _dot(tri, jnp.concatenate(_split3(logf), axis=1))
    cum = parts[:, :128] + parts[:, 128:256] + parts[:, 256:] + carry_ref[...]
    carry_ref[...] = cum[tm - 1:tm, :]
    pr = lax.broadcasted_iota(jnp.int32, (3 * 128, _GROUP), 0)
    pc = lax.broadcasted_iota(jnp.int32, (3 * 128, _GROUP), 1)
    term, hd = pr // 128, pr % 128
    place = pc - hd * HEAD_PAD - jnp.where(hd % 2 == 0, QK_DIM, 0) == term
    fk_bias = _dot(jnp.concatenate(_split3(cum * (-LOG2E)), axis=1), place.astype(BF16))
    hq = proj(OFF_FQ, _FOX)
    hk = proj(OFF_FK, _FOX)
    hv = proj(OFF_FV, _FOX)
    for h in range(HEADS):
        pair, half = divmod(h, 2)
        keep = in_half[half]
        fq_ref[0, h] = jnp.where(keep, head(hq, pair) * qk_scale, ones3[half]).astype(BF16)
        fk_ref[0, h] = jnp.where(keep, head(hk, pair), head(fk_bias, h)).astype(BF16)
        fvt_ref[0, h, 0] = v_rows(jnp.where(keep, head(hv, pair), one1[half]), half).astype(BF16)

    pos = (i * tm + lax.broadcasted_iota(jnp.int32, (tm, HEAD_PAD), 0)).astype(F32)
    hq = proj(OFF_DQ, _GROUP)
    hk = proj(OFF_DK, _GROUP)
    hv = proj(OFF_DV, _GROUP)
    ones_rows = jnp.where(lax.broadcasted_iota(jnp.int32, (BF16_ROWS, tm), 0) == 0, 1.0, 0.0)
    for h in range(HEADS):
        slope =2.0 ** (-8.0 * (h + 1) / HEADS)
        a_hi, a_mid, a_lo = _split3(pos * (slope * LOG2E))
        kbias = jnp.where(extra == 0, a_hi.astype(F32),
                          jnp.where(extra == 1, a_mid.astype(F32),
                                    jnp.where(extra == 2, a_lo.astype(F32), 0.0)))
        q = head(hq, h) * qk_scale
        k = head(hk, h)
        for half, (q_ref, k_ref) in enumerate(((dq1_ref, dk1_ref), (dq2_ref, dk2_ref))):
            q_ref[0, h] = jnp.where(in_half[half], q, ones3[half]).astype(BF16)
            k_ref[0, h] = jnp.where(in_half[half], k, kbias).astype(BF16)
        dvt_ref[0, h, 0] = jnp.concatenate([head(hv, h).T, ones_rows], axis=0).astype(BF16)

    cqn = _rms(proj(OFF_CQ, MLA_Q_RANK), qng_ref[...]).astype(BF16)
    ckvn = _rms(proj(OFF_CKV, MLA_KV_RANK), kvng_ref[...]).astype(BF16)
    cos = cos_ref[...]
    sin = sin_ref[...]
    krot = proj(OFF_KR, 128) * cos + proj(OFF_KRP, 128) * sin
    qa = _dot(cqn, wuq_ref[:, :_GROUP])
    qb = _dot(cqn, wuq_ref[:, _GROUP:])
    kk = _dot(ckvn, wukv_ref[:, :_GROUP])
    vv = _dot(ckvn, wukv_ref[:, _GROUP:])
    mla_scale = (QK_DIM + MLA_ROPE) ** -0.5 * LOG2E
    for h in range(HEADS):
        mq_ref[0, h] = ((head(qa, h) * cos + head(qb, h) * sin) * mla_scale).astype(BF16)
        mk_ref[0, h] = (head(kk, h) + krot).astype(BF16)
        mvt_ref[0, h, 0] = v_rows(head(vv, h) + one1[0], 0).astype(BF16)


def _project(x, layer, g1, wa, fb, qng, wuq, kvng, wukv, cos_t, sin_t):
    b, s, _ = x.shape
    per_layer = functools.partial(_layer_spec, layer=layer)
    tm = TILE_PROJ
    rows = (jax.ShapeDtypeStruct((b, HEADS, s, HEAD_PAD), BF16),
            pl.BlockSpec((1, HEADS, tm, HEAD_PAD), lambda bi, i: (bi, 0, i, 0)))
    def cols(v_rows):
        return (jax.ShapeDtypeStruct((b, HEADS, s // tm, v_rows, tm), BF16),
                pl.BlockSpec((1, HEADS, 1, v_rows, tm), lambda bi, i: (bi, 0, i, 0, 0)))

    outs = [rows, rows, cols(V_ROWS_ONES), rows, rows, rows, rows, cols(V_ROWS_DIFF),
            rows, rows, cols(V_ROWS_ONES)]
    row_tab = pl.BlockSpec((tm, HEAD_PAD), lambda bi, i: (i, 0))
    return pl.pallas_call(
        functools.partial(_proj_kernel, tm=tm),
        grid=(b, s // tm),
        in_specs=[
            pl.BlockSpec((1, tm, D_MODEL), lambda bi, i: (bi, i, 0)),
            per_layer((1, D_MODEL)),
            per_layer((D_MODEL, PROJ_COLS)),
            per_layer((1, 128)),
            per_layer((1, MLA_Q_RANK)),
            per_layer((MLA_Q_RANK, 2 * _GROUP)),
            per_layer((1, MLA_KV_RANK)),
            per_layer((MLA_KV_RANK, 2 * _GROUP)),
            row_tab, row_tab,
        ],
        out_specs=[o[1] for o in outs],
        out_shape=[o[0] for o in outs],
        scratch_shapes=[pltpu.VMEM((1, 128), F32)],
        compiler_params=pltpu.CompilerParams(
            dimension_semantics=("arbitrary", "arbitrary"), vmem_limit_bytes=VMEM_LIMIT),
        name="proj",
    )(x, g1, wa, fb, qng, wuq, kvng, wukv, cos_t, sin_t)


def _tile_ids(t):
    key = lax.broadcasted_iota(jnp.int32, (t, t), 0)
    qry = lax.broadcasted_iota(jnp.int32, (t, t), 1)
    return key, qry


def _online_step(s, vt, m_ref, acc_ref, lanes=slice(None)):
    m_prev = m_ref[:, lanes]
    m_new = jnp.maximum(m_prev, jnp.max(s, axis=0, keepdims=True))
    p = jnp.exp2(s - m_new)
    alpha = jnp.exp2(m_prev - m_new)
    acc_ref[:, lanes] = alpha * acc_ref[:, lanes] + _dot(vt, p.astype(BF16))
    m_ref[:, lanes] = m_new


class _Map(NamedTuple):
    q_ref: object
    q_next_ref: object
    k_ref: object
    vt_ref: object
    bias_ref: object
    m_ref: object
    acc_ref: object


def _flash(i, maps, buf_a, buf_b, tk):
    for mp in maps:
        mp.m_ref[...] = jnp.full_like(mp.m_ref, MASKED)
        mp.acc_ref[...] = jnp.zeros_like(mp.acc_ref)

    def k_tile(mp, kt):
        return mp.k_ref[0, 0, pl.ds(pl.multiple_of(kt * tk, tk), tk), :]

    def compute_scores(kt, buf):
        for mp, s_ref in zip(maps, buf):
            s_ref[...] = _dot_nt(k_tile(mp, kt), mp.q_ref[0, 0])

    def consume(kt, buf):
        for mp, s_ref in zip(maps, buf):
            _online_step(s_ref[...], mp.vt_ref[0, 0, kt], mp.m_ref, mp.acc_ref)

    @pl.when(i == 0)
    def _():
        compute_scores(0, buf_a)

    def pair(j):
        compute_scores(j + 1, buf_b)
        consume(j, buf_a)
        compute_scores(j + 2, buf_a)
        consume(j + 1, buf_b)

    odd = jnp.bitwise_and(i, 1)

    @pl.when(odd == 1)
    def _():
        pair(0)

    def body(jj, c):
        j = 2 * odd + 4 * jj
        pair(j)
        pair(j + 2)
        return c

    lax.fori_loop(0, lax.shift_right_logical(i, 1), body, 0)

    lower, upper = slice(0, tk), slice(tk, None)
    for mp, sb_ref in zip(maps, buf_b):
        sb_ref[:, lower] = _dot_nt(k_tile(mp, 2 * i + 1), mp.q_ref[0, 0, upper, :])
    for mp, sa_ref, sb_ref in zip(maps, buf_a, buf_b):
        bias = mp.bias_ref[...]
        vt0, vt1 = mp.vt_ref[0, 0, 2 * i], mp.vt_ref[0, 0, 2 * i + 1]
        _online_step(sa_ref[:, lower] + bias, vt0, mp.m_ref, mp.acc_ref, lanes=lower)
        s = jnp.concatenate([sa_ref[:, upper], sb_ref[:, lower] + bias], axis=0)
        _online_step(s, jnp.concatenate([vt0, vt1], axis=1), mp.m_ref, mp.acc_ref, lanes=upper)

    for mp, sa_ref in zip(maps, buf_a):
        sa_ref[...] = _dot_nt(k_tile(mp, 0), mp.q_next_ref[0, 0])


def _normalized(acc_ref, rows):
    acc = acc_ref[...]
    return acc[:rows] * (1.0 / acc[rows:rows + 1, :])


def _diag_bias(tk, chunked, slope2=None):
    key, qry = _tile_ids(tk)
    keep = (key // CHUNK) <= (qry // CHUNK) if chunked else key <= qry
    if slope2 is None:
        return jnp.where(keep, 0.0, MASKED)
    return jnp.where(keep, jnp.maximum(key - qry, 0).astype(F32) * -slope2, MASKED)


def _fox_mla_kernel(fq_ref, fqn_ref, fk_ref, fvt_ref, mq_ref, mqn_ref, mk_ref, mvt_ref, o_ref,
                    sfa_ref, sma_ref, sfb_ref, smb_ref, fbias_ref, mbias_ref,
                    fm_ref, facc_ref, mm_ref, macc_ref, *, tk):
    i = pl.program_id(2)

    @pl.when(i == 0)
    def _():
        fbias_ref[...] = _diag_bias(tk, False)
        mbias_ref[...] = _diag_bias(tk, True)

    maps = (_Map(fq_ref, fqn_ref, fk_ref, fvt_ref, fbias_ref, fm_ref, facc_ref),
            _Map(mq_ref, mqn_ref, mk_ref, mvt_ref, mbias_ref, mm_ref, macc_ref))
    _flash(i, maps, (sfa_ref, sma_ref), (sfb_ref, smb_ref), tk)

    o_t = jnp.concatenate([_normalized(mp.acc_ref, QK_DIM) for mp in maps], axis=0)
    o_ref[0] = o_t.T.astype(BF16)


def _diff_kernel(q1_ref, q1n_ref, q2_ref, q2n_ref, k1_ref, k2_ref, vt_ref, lam_ref, g_ref, o_ref,
                 s1a_ref, s2a_ref, s1b_ref, s2b_ref, bias_ref,
                 m1_ref, acc1_ref, m2_ref, acc2_ref, *, tk, lam_init):
    h = pl.program_id(1)
    i = pl.program_id(2)

    @pl.when(i == 0)
    def _():
        slope2 = jnp.exp2(jnp.full((1, 1), -2.0, F32) * (h + 1).astype(F32)) * (2.0 * LOG2E)
        bias_ref[...] = _diag_bias(tk, True, slope2)

    maps = (_Map(q1_ref, q1n_ref, k1_ref, vt_ref, bias_ref, m1_ref, acc1_ref),
            _Map(q2_ref, q2n_ref, k2_ref, vt_ref, bias_ref, m2_ref, acc2_ref))
    _flash(i, maps, (s1a_ref, s2a_ref), (s1b_ref, s2b_ref), tk)

    lams = lam_ref[...]
    lam = (jnp.exp(jnp.sum(lams[0:1] * lams[1:2], axis=-1, keepdims=True))
           - jnp.exp(jnp.sum(lams[2:3] * lams[3:4], axis=-1, keepdims=True)) + lam_init)
    o_t = _normalized(acc1_ref, HEAD_PAD) - lam * _normalized(acc2_ref, HEAD_PAD)
    o_ref[0] = (_rms(o_t.T, g_ref[...]) * (1.0 - lam_init)).astype(BF16)


def _attn_specs(s, v_rows):
    tq, tk = TILE_Q, TILE_K
    last = s // tq - 1
    q_spec = pl.BlockSpec((1, 1, tq, HEAD_PAD), lambda b, h, i: (b, h, i, 0))
    q_next_spec = pl.BlockSpec((1, 1, tq, HEAD_PAD),
                               lambda b, h, i: (b, h, jnp.minimum(i + 1, last), 0))
    k_spec = pl.BlockSpec((1, 1, s, HEAD_PAD), lambda b, h, i: (b, h, 0, 0))
    vt_spec = pl.BlockSpec((1, 1, s // tk, v_rows, tk), lambda b, h, i: (b, h, 0, 0, 0))
    o_spec = pl.BlockSpec((1, tq, HEAD_PAD), lambda b, h, i: (b, i, h))
    return (q_spec, q_next_spec), k_spec, vt_spec, o_spec


def _fox_mla_attention(fq, fk, fvt, mq, mk, mvt):
    b, _, s, _ = fq.shape
    tq, tk = TILE_Q, TILE_K
    q_spec, k_spec, vt_spec, o_spec = _attn_specs(s, V_ROWS_ONES)
    score = pltpu.VMEM((tk, tq), F32)
    bias = pltpu.VMEM((tk, tk), F32)
    stat = pltpu.VMEM((1, tq), F32)
    acc = pltpu.VMEM((V_ROWS_ONES, tq), F32)
    return pl.pallas_call(
        functools.partial(_fox_mla_kernel, tk=tk),
        grid=(b, HEADS, s // tq),
        in_specs=[*q_spec, k_spec, vt_spec, *q_spec, k_spec, vt_spec],
        out_specs=o_spec,
        out_shape=jax.ShapeDtypeStruct((b, s, _GROUP), BF16),
        scratch_shapes=[score, score, score, score, bias, bias, stat, acc, stat, acc],
        compiler_params=pltpu.CompilerParams(
            dimension_semantics=("arbitrary",) * 3, vmem_limit_bytes=VMEM_LIMIT),
        name="fox_mla_attn",
    )(fq, fq, fk, fvt, mq, mq, mk, mvt)


def _diff_attention(q1, q2, k1, k2, vt, layer, lams, g, lam_init):
    b, _, s, _ = q1.shape
    tq, tk = TILE_Q, TILE_K
    q_spec, k_spec, vt_spec, o_spec = _attn_specs(s, V_ROWS_DIFF)
    score = pltpu.VMEM((tk, tq), F32)
    stat = pltpu.VMEM((1, tq), F32)
    acc = pltpu.VMEM((V_ROWS_DIFF, tq), F32)
    return pl.pallas_call(
        functools.partial(_diff_kernel, tk=tk, lam_init=lam_init),
        grid=(b, HEADS, s // tq),
        in_specs=[*q_spec, *q_spec, k_spec, k_spec, vt_spec,
                  _layer_spec((4, QK_DIM), layer), _layer_spec((1, HEAD_PAD), layer)],
        out_specs=o_spec,
        out_shape=jax.ShapeDtypeStruct((b, s, _GROUP), BF16),
        scratch_shapes=[score, score, score, score, pltpu.VMEM((tk, tk), F32),
                        stat, acc, stat, acc],
        compiler_params=pltpu.CompilerParams(
            dimension_semantics=("arbitrary",) * 3, vmem_limit_bytes=VMEM_LIMIT),
        name="diff_attn",
    )(q1, q1, q2, q2, k1, k2, vt, lams, g)


def _ffn_kernel(x_ref, ofm_ref, od_ref, wofm_ref, wod_ref, g2_ref,
                wup_ref, cw_ref, cb_ref, wdn_ref, gf_ref, o_ref, xn_ref, act_ref,
                *, tm, final):
    i = pl.program_id(1)

    @pl.when(i == 0)
    def _():
        xn_ref[0:HALO, :] = jnp.zeros((HALO, D_MODEL), BF16)

    @pl.when(i > 0)
    def _():
        xn_ref[0:HALO, :] = xn_ref[tm:tm + HALO, :]

    x1 = x_ref[0] + _dot(ofm_ref[0], wofm_ref[...]) + _dot(od_ref[0], wod_ref[...])
    xn_ref[HALO:HALO + tm, :] = _rms(x1, g2_ref[...]).astype(BF16)

    def conv(up, off):
        cw = cw_ref[:, off:off + FF_CHUNK]
        y = cb_ref[:, off:off + FF_CHUNK]
        for j in range(CONV_W):
            back = CONV_W - 1 - j
            tap = up if back == 0 else pltpu.roll(up, back, axis=0)
            y = y + tap[HALO:HALO + tm, :] * cw[j:j + 1, :]
        return y

    xn = xn_ref[...]
    for c in range(D_FF // FF_CHUNK):
        og = c * FF_CHUNK
        ou = D_FF + og
        yg = conv(_dot(xn, wup_ref[:, og:og + FF_CHUNK]), og)
        yu = conv(_dot(xn, wup_ref[:, ou:ou + FF_CHUNK]), ou)
        act = yg * (1.0 / (1.0 + jnp.exp(-yg))) * yu
        act_ref[:, og:og + FF_CHUNK] = act.astype(BF16)

    x2 = x1 + _dot(act_ref[...], wdn_ref[...])
    if final:
        x2 = _rms(x2, gf_ref[...])
    o_ref[0] = x2


def _ffn(x, o_fm, o_diff, layer, wo_fm, wo_d, g2, w_up, conv_w, conv_b, w_down, g_final, final):
    b, s, _ = x.shape
    per_layer = functools.partial(_layer_spec, layer=layer)
    tm = TILE_FFN
    row = lambda width: pl.BlockSpec((1, tm, width), lambda bi, i: (bi, i, 0))
    return pl.pallas_call(
        functools.partial(_ffn_kernel, tm=tm, final=final),
        grid=(b, s // tm),
        in_specs=[
            row(D_MODEL), row(_GROUP), row(_GROUP),
            per_layer((_GROUP, D_MODEL)), per_layer((_GROUP, D_MODEL)),
            per_layer((1, D_MODEL)),
            per_layer((D_MODEL, 2 * D_FF)),
            per_layer((CONV_W, 2 * D_FF)),
            per_layer((1, 2 * D_FF)),
            per_layer((D_FF, D_MODEL)),
            _const_spec((1, D_MODEL)),
        ],
        out_specs=row(D_MODEL),
        out_shape=jax.ShapeDtypeStruct((b, s, D_MODEL), F32),
        scratch_shapes=[pltpu.VMEM((HALO + tm, D_MODEL), BF16), pltpu.VMEM((tm, D_FF), BF16)],
        compiler_params=pltpu.CompilerParams(
            dimension_semantics=("arbitrary", "arbitrary"), vmem_limit_bytes=VMEM_LIMIT),
        name="ffn",
    )(x, o_fm, o_diff, wo_fm, wo_d, g2, w_up, conv_w, conv_b, w_down, g_final)


def _rot_half_cols(w):
    half = MLA_ROPE // 2
    return jnp.concatenate([w[..., half:], w[..., :half]], axis=-1)


def _layout_w_in(w):
    n_ff = OFF_FF + HEADS
    n_kr = w.shape[-1] - MLA_ROPE
    zeros = lambda n: jnp.zeros(w.shape[:-1] + (n,), w.dtype)
    kr = w[..., n_kr:]
    rope_pad = HEAD_PAD - QK_DIM - MLA_ROPE
    out = jnp.concatenate(
        [w[..., :n_ff], zeros(128 - HEADS), w[..., n_ff:n_kr],
         zeros(QK_DIM), kr, zeros(rope_pad),
         zeros(QK_DIM), _rot_half_cols(kr), zeros(rope_pad)], axis=-1).astype(BF16)
    assert out.shape[-1] == PROJ_COLS
    return out


def _layout_w_uq(w):
    w = w.reshape(DEPTH, MLA_Q_RANK, HEADS, QK_DIM + MLA_ROPE)
    nope, rope = w[..., :QK_DIM], w[..., QK_DIM:]
    pad = jnp.zeros(w.shape[:-1] + (HEAD_PAD - QK_DIM - MLA_ROPE,), w.dtype)
    direct = jnp.concatenate([nope, rope, pad], axis=-1)
    swapped = jnp.concatenate([jnp.zeros_like(nope), _rot_half_cols(rope), pad], axis=-1)
    return jnp.concatenate([direct.reshape(DEPTH, MLA_Q_RANK, -1),
                            swapped.reshape(DEPTH, MLA_Q_RANK, -1)], axis=-1).astype(BF16)


def _layout_w_ukv(w):
    w = w.reshape(DEPTH, MLA_KV_RANK, HEADS, 2 * QK_DIM)
    pad = jnp.zeros(w.shape[:-1] + (HEAD_PAD - QK_DIM,), w.dtype)
    k = jnp.concatenate([w[..., :QK_DIM], pad], axis=-1).reshape(DEPTH, MLA_KV_RANK, -1)
    v = jnp.concatenate([w[..., QK_DIM:], pad], axis=-1).reshape(DEPTH, MLA_KV_RANK, -1)
    return jnp.concatenate([k, v], axis=-1).astype(BF16)


def _interleave_head_rows(w_fox, w_mla):
    parts = [w.reshape(DEPTH, HEADS, QK_DIM, -1) for w in (w_fox, w_mla)]
    return jnp.concatenate(parts, axis=2).reshape(DEPTH, HEADS * HEAD_PAD, -1)


def _rope_tables(s):
    half = MLA_ROPE // 2
    inv = 1.0 / (ROPE_THETA ** (jnp.arange(half, dtype=F32) / half))
    pos = (jnp.arange(s * half, dtype=jnp.int32) // half).astype(F32)
    ang = pos * jnp.tile(inv, s)
    cos, sin = jnp.cos(ang).reshape(s, half), jnp.sin(ang).reshape(s, half)
    tail = jnp.zeros((s, HEAD_PAD - QK_DIM - MLA_ROPE), F32)
    cos_t = jnp.concatenate([jnp.ones((s, QK_DIM), F32), cos, cos, tail], axis=1)
    sin_t = jnp.concatenate([jnp.zeros((s, QK_DIM), F32), -sin, sin, tail], axis=1)
    return cos_t, sin_t


def kernel(x, ln1_g, w_in, fgate_b, lam_q1, lam_k1, lam_q2, lam_k2, diff_norm_g, q_norm_g, w_uq,
           kv_norm_g, w_ukv, w_o, ln2_g, w_up, conv_w, conv_b, w_down, final_g):
    s = x.shape[1]
    cos_t, sin_t = _rope_tables(s)
    row = lambda v: v.reshape(DEPTH, 1, -1)
    wa = _layout_w_in(w_in)
    fb = row(jnp.pad(fgate_b, ((0, 0), (0, 128 - HEADS))))
    wuq = _layout_w_uq(w_uq)
    wukv = _layout_w_ukv(w_ukv)
    lams = jnp.stack([lam_q1, lam_k1, lam_q2, lam_k2], axis=1)
    wo = w_o.astype(BF16)
    wo_fm = _interleave_head_rows(wo[:, :_FOX], wo[:, _FOX + _GROUP:])
    wo_d = wo[:, _FOX:_FOX + _GROUP]
    wup = w_up.astype(BF16)
    wdn = w_down.astype(BF16)
    g1, g2, qng, kvng, dng, cb = (row(v) for v in (ln1_g, ln2_g, q_norm_g, kv_norm_g,
                                                   diff_norm_g, conv_b))
    for i in range(DEPTH):
        lam_init = 0.8 - 0.6 * math.exp(-0.3 * i)
        heads = _project(x, i, g1, wa, fb, qng, wuq, kvng, wukv, cos_t, sin_t)
        fq, fk, fvt, dq1, dq2, dk1, dk2, dvt, mq, mk, mvt = heads
        o_fm = _fox_mla_attention(fq, fk, fvt, mq, mk, mvt)
        o_diff = _diff_attention(dq1, dq2, dk1, dk2, dvt, i, lams, dng, lam_init)
        x = _ffn(x, o_fm, o_diff, i, wo_fm, wo_d, g2, wup, conv_w, cb, wdn,
                 final_g.reshape(1, -1), final=(i == DEPTH - 1))
    return x
```

```python
import functools
import math
from typing import NamedTuple

import jax
import jax.numpy as jnp
from jax import lax
from jax.experimental import pallas as pl
from jax.experimental.pallas import tpu as pltpu

F32 = jnp.float32
BF16 = jnp.bfloat16

D_MODEL = 1024
DEPTH = 2
CHUNK = 64
EPS = 1e-6
MASKED = -1e30

HEADS = 4
HEAD_PAD = 128
QK_DIM = 64
MLA_ROPE = 32
MLA_Q_RANK = 256
MLA_KV_RANK = 128
ROPE_THETA = 10000.0
D_FF = 2816
CONV_W = 3

LOG2E = 1.4426950408889634

_GROUP = HEADS * HEAD_PAD
_FOX = HEADS * QK_DIM
OFF_FQ, OFF_FK, OFF_FV = 0, _FOX, 2 * _FOX
OFF_FF = 3 * _FOX
OFF_DQ = OFF_FF + 128
OFF_DK = OFF_DQ + _GROUP
OFF_DV = OFF_DK + _GROUP
OFF_CQ = OFF_DV + _GROUP
OFF_CKV = OFF_CQ + MLA_Q_RANK
OFF_KR = OFF_CKV + MLA_KV_RANK
OFF_KRP = OFF_KR + 128
PROJ_COLS = OFF_KRP + 128

TILE_K = 512
TILE_Q = 2 * TILE_K
TILE_PROJ = TILE_K
BF16_ROWS = 16
V_ROWS_ONES = QK_DIM + BF16_ROWS
V_ROWS_DIFF = HEAD_PAD + BF16_ROWS
TILE_FFN = 512
FF_CHUNK = 256
HALO = 16
VMEM_LIMIT = 56 * 1024 * 1024


def _dot(a, b):
    return jnp.dot(a, b, preferred_element_type=F32)


def _dot_nt(a, b):
    return lax.dot_general(a, b, (((1,), (1,)), ((), ())), preferred_element_type=F32)


def _rms(x, g):
    return x * lax.rsqrt(jnp.mean(x * x, axis=-1, keepdims=True) + EPS) * g


def _split3(x):
    hi = x.astype(BF16)
    r1 = x - hi.astype(F32)
    mid = r1.astype(BF16)
    lo = (r1 - mid.astype(F32)).astype(BF16)
    return hi, mid, lo


def _const_spec(shape):
    nd = len(shape)
    return pl.BlockSpec(shape, lambda *_: (0,) * nd, pipeline_mode=pl.Buffered(1))


def _layer_spec(shape, layer):
    nd = len(shape)
    return pl.BlockSpec((None,) + tuple(shape), lambda *_: (layer,) + (0,) * nd,
                        pipeline_mode=pl.Buffered(1))


def _proj_kernel(x_ref, g1_ref, wa_ref, fb_ref, qng_ref, wuq_ref, kvng_ref, wukv_ref,
                 cos_ref, sin_ref,
                 fq_ref, fk_ref, fvt_ref, dq1_ref, dq2_ref, dk1_ref, dk2_ref, dvt_ref,
                 mq_ref, mk_ref, mvt_ref, carry_ref, *, tm):
    i = pl.program_id(1)

    @pl.when(i == 0)
    def _():
        carry_ref[...] = jnp.zeros_like(carry_ref)

    xn = _rms(x_ref[0], g1_ref[...]).astype(BF16)

    def proj(off, width):
        return _dot(xn, wa_ref[:, off:off + width])

    def head(t, h):
        return t[:, h * HEAD_PAD:(h + 1) * HEAD_PAD]

    lane = lax.broadcasted_iota(jnp.int32, (tm, HEAD_PAD), 1)
    low = lane < QK_DIM
    extra = jnp.where(low, lane, lane - QK_DIM)
    in_half = (low, jnp.logical_not(low))
    ones3 = tuple(jnp.where(jnp.logical_not(m) & (extra < 3), 1.0, 0.0) for m in in_half)
    one1 = tuple(jnp.where(jnp.logical_not(m) & (extra == 0), 1.0, 0.0) for m in in_half)
    qk_scale = QK_DIM ** -0.5 * LOG2E

    def v_rows(t, half):
        t = t.T
        if half == 0:
            return t[:V_ROWS_ONES]
        return jnp.concatenate([t[QK_DIM:], t[:V_ROWS_ONES - QK_DIM]], axis=0)

    z = proj(OFF_FF, 128) + fb_ref[...]
    logf = jnp.minimum(z, 0.0) - jnp.log1p(jnp.exp(-jnp.abs(z)))
    rr = lax.broadcasted_iota(jnp.int32, (tm, tm), 0)
    cc = lax.broadcasted_iota(jnp.int32, (tm, tm), 1)
    tri = (cc <= rr).astype(BF16)
    parts = _dot(tri, jnp.concatenate(_split3(logf), axis=1))
    cum = parts[:, :128] + parts[:, 128:256] + parts[:, 256:] + carry_ref[...]
    carry_ref[...] = cum[tm - 1:tm, :]
    pr = lax.broadcasted_iota(jnp.int32, (3 * 128, _GROUP), 0)
    pc = lax.broadcasted_iota(jnp.int32, (3 * 128, _GROUP), 1)
    term, hd = pr // 128, pr % 128
    place = pc - hd * HEAD_PAD - jnp.where(hd % 2 == 0, QK_DIM, 0) == term
    fk_bias = _dot(jnp.concatenate(_split3(cum * (-LOG2E)), axis=1), place.astype(BF16))
    hq = proj(OFF_FQ, _FOX)
    hk = proj(OFF_FK, _FOX)
    hv = proj(OFF_FV, _FOX)
    for h in range(HEADS):
        pair, half = divmod(h, 2)
        keep = in_half[half]
        fq_ref[0, h] = jnp.where(keep, head(hq, pair) * qk_scale, ones3[half]).astype(BF16)
        fk_ref[0, h] = jnp.where(keep, head(hk, pair), head(fk_bias, h)).astype(BF16)
        fvt_ref[0, h, 0] = v_rows(jnp.where(keep, head(hv, pair), one1[half]), half).astype(BF16)

    pos = (i * tm + lax.broadcasted_iota(jnp.int32, (tm, HEAD_PAD), 0)).astype(F32)
    hq = proj(OFF_DQ, _GROUP)
    hk = proj(OFF_DK, _GROUP)
    hv = proj(OFF_DV, _GROUP)
    ones_rows = jnp.where(lax.broadcasted_iota(jnp.int32, (BF16_ROWS, tm), 0) == 0, 1.0, 0.0)
    for h in range(HEADS):
        slope = 2.0 ** (-8.0 * (h + 1) / HEADS)
        a_hi, a_mid, a_lo = _split3(pos * (slope * LOG2E))
        kbias = jnp.where(extra == 0, a_hi.astype(F32),
                          jnp.where(extra == 1, a_mid.astype(F32),
                                    jnp.where(extra == 2, a_lo.astype(F32), 0.0)))
        q = head(hq, h) * qk_scale
        k = head(hk, h)
        for half, (q_ref, k_ref) in enumerate(((dq1_ref, dk1_ref), (dq2_ref, dk2_ref))):
            q_ref[0, h] = jnp.where(in_half[half], q, ones3[half]).astype(BF16)
            k_ref[0, h] = jnp.where(in_half[half], k, kbias).astype(BF16)
        dvt_ref[0, h, 0] = jnp.concatenate([head(hv, h).T, ones_rows], axis=0).astype(BF16)

    cqn = _rms(proj(OFF_CQ, MLA_Q_RANK), qng_ref[...]).astype(BF16)
    ckvn = _rms(proj(OFF_CKV, MLA_KV_RANK), kvng_ref[...]).astype(BF16)
    cos = cos_ref[...]
    sin = sin_ref[...]
    krot = proj(OFF_KR, 128) * cos + proj(OFF_KRP, 128) * sin
    qa = _dot(cqn, wuq_ref[:, :_GROUP])
    qb = _dot(cqn, wuq_ref[:, _GROUP:])
    kk = _dot(ckvn, wukv_ref[:, :_GROUP])
    vv = _dot(ckvn, wukv_ref[:, _GROUP:])
    mla_scale = (QK_DIM + MLA_ROPE) ** -0.5 * LOG2E
    for h in range(HEADS):
        mq_ref[0, h] = ((head(qa, h) * cos + head(qb, h) * sin) * mla_scale).astype(BF16)
        mk_ref[0, h] = (head(kk, h) + krot).astype(BF16)
        mvt_ref[0, h, 0] = v_rows(head(vv, h) + one1[0], 0).astype(BF16)


def _project(x, layer, g1, wa, fb, qng, wuq, kvng, wukv, cos_t, sin_t):
    b, s, _ = x.shape
    per_layer = functools.partial(_layer_spec, layer=layer)
    tm = TILE_PROJ
    rows = (jax.ShapeDtypeStruct((b, HEADS, s, HEAD_PAD), BF16),
            pl.BlockSpec((1, HEADS, tm, HEAD_PAD), lambda bi, i: (bi, 0, i, 0)))
    def cols(v_rows):
        return (jax.ShapeDtypeStruct((b, HEADS, s // tm, v_rows, tm), BF16),
                pl.BlockSpec((1, HEADS, 1, v_rows, tm), lambda bi, i: (bi, 0, i, 0, 0)))

    outs = [rows, rows, cols(V_ROWS_ONES), rows, rows, rows, rows, cols(V_ROWS_DIFF),
            rows, rows, cols(V_ROWS_ONES)]
    row_tab = pl.BlockSpec((tm, HEAD_PAD), lambda bi, i: (i, 0))
    return pl.pallas_call(
        functools.partial(_proj_kernel, tm=tm),
        grid=(b, s // tm),
        in_specs=[
            pl.BlockSpec((1, tm, D_MODEL), lambda bi, i: (bi, i, 0)),
            per_layer((1, D_MODEL)),
            per_layer((D_MODEL, PROJ_COLS)),
            per_layer((1, 128)),
            per_layer((1, MLA_Q_RANK)),
            per_layer((MLA_Q_RANK, 2 * _GROUP)),
            per_layer((1, MLA_KV_RANK)),
            per_layer((MLA_KV_RANK, 2 * _GROUP)),
            row_tab, row_tab,
        ],
        out_specs=[o[1] for o in outs],
        out_shape=[o[0] for o in outs],
        scratch_shapes=[pltpu.VMEM((1, 128), F32)],
        compiler_params=pltpu.CompilerParams(
            dimension_semantics=("arbitrary", "arbitrary"), vmem_limit_bytes=VMEM_LIMIT),
        name="proj",
    )(x, g1, wa, fb, qng, wuq, kvng, wukv, cos_t, sin_t)


def _tile_ids(t):
    key = lax.broadcasted_iota(jnp.int32, (t, t), 0)
    qry = lax.broadcasted_iota(jnp.int32, (t, t), 1)
    return key, qry


def _online_step(s, vt, m_ref, acc_ref, lanes=slice(None)):
    m_prev = m_ref[:, lanes]
    m_new = jnp.maximum(m_prev, jnp.max(s, axis=0, keepdims=True))
    p = jnp.exp2(s - m_new)
    alpha = jnp.exp2(m_prev - m_new)
    acc_ref[:, lanes] = alpha * acc_ref[:, lanes] + _dot(vt, p.astype(BF16))
    m_ref[:, lanes] = m_new


class _Map(NamedTuple):
    q_ref: object
    q_next_ref: object
    k_ref: object
    vt_ref: object
    bias_ref: object
    m_ref: object
    acc_ref: object


def _flash(i, maps, buf_a, buf_b, tk):
    for mp in maps:
        mp.m_ref[...] = jnp.full_like(mp.m_ref, MASKED)
        mp.acc_ref[...] = jnp.zeros_like(mp.acc_ref)

    def k_tile(mp, kt):
        return mp.k_ref[0, 0, pl.ds(pl.multiple_of(kt * tk, tk), tk), :]

    def compute_scores(kt, buf):
        for mp, s_ref in zip(maps, buf):
            s_ref[...] = _dot_nt(k_tile(mp, kt), mp.q_ref[0, 0])

    def consume(kt, buf):
        for mp, s_ref in zip(maps, buf):
            _online_step(s_ref[...], mp.vt_ref[0, 0, kt], mp.m_ref, mp.acc_ref)

    @pl.when(i == 0)
    def _():
        compute_scores(0, buf_a)

    def pair(j):
        compute_scores(j + 1, buf_b)
        consume(j, buf_a)
        compute_scores(j + 2, buf_a)
        consume(j + 1, buf_b)

    odd = jnp.bitwise_and(i, 1)

    @pl.when(odd == 1)
    def _():
        pair(0)

    def body(jj, c):
        j = 2 * odd + 4 * jj
        pair(j)
        pair(j + 2)
        return c

    lax.fori_loop(0, lax.shift_right_logical(i, 1), body, 0)

    lower, upper = slice(0, tk), slice(tk, None)
    for mp, sb_ref in zip(maps, buf_b):
        sb_ref[:, lower] = _dot_nt(k_tile(mp, 2 * i + 1), mp.q_ref[0, 0, upper, :])
    for mp, sa_ref, sb_ref in zip(maps, buf_a, buf_b):
        bias = mp.bias_ref[...]
        vt0, vt1 = mp.vt_ref[0, 0, 2 * i], mp.vt_ref[0, 0, 2 * i + 1]
        _online_step(sa_ref[:, lower] + bias, vt0, mp.m_ref, mp.acc_ref, lanes=lower)
        s = jnp.concatenate([sa_ref[:, upper], sb_ref[:, lower] + bias], axis=0)
        _online_step(s, jnp.concatenate([vt0, vt1], axis=1), mp.m_ref, mp.acc_ref, lanes=upper)

    for mp, sa_ref in zip(maps, buf_a):
        sa_ref[...] = _dot_nt(k_tile(mp, 0), mp.q_next_ref[0, 0])


def _normalized(acc_ref, rows):
    acc = acc_ref[...]
    return acc[:rows] * (1.0 / acc[rows:rows + 1, :])


def _diag_bias(tk, chunked, slope2=None):
    key, qry = _tile_ids(tk)
    keep = (key // CHUNK) <= (qry // CHUNK) if chunked else key <= qry
    if slope2 is None:
        return jnp.where(keep, 0.0, MASKED)
    return jnp.where(keep, jnp.maximum(key - qry, 0).astype(F32) * -slope2, MASKED)


def _fox_mla_kernel(fq_ref, fqn_ref, fk_ref, fvt_ref, mq_ref, mqn_ref, mk_ref, mvt_ref, o_ref,
                    sfa_ref, sma_ref, sfb_ref, smb_ref, fbias_ref, mbias_ref,
                    fm_ref, facc_ref, mm_ref, macc_ref, *, tk):
    i = pl.program_id(2)

    @pl.when(i == 0)
    def _():
        fbias_ref[...] = _diag_bias(tk, False)
        mbias_ref[...] = _diag_bias(tk, True)

    maps = (_Map(fq_ref, fqn_ref, fk_ref, fvt_ref, fbias_ref, fm_ref, facc_ref),
            _Map(mq_ref, mqn_ref, mk_ref, mvt_ref, mbias_ref, mm_ref, macc_ref))
    _flash(i, maps, (sfa_ref, sma_ref), (sfb_ref, smb_ref), tk)

    o_t = jnp.concatenate([_normalized(mp.acc_ref, QK_DIM) for mp in maps], axis=0)
    o_ref[0] = o_t.T.astype(BF16)


def _diff_kernel(q1_ref, q1n_ref, q2_ref, q2n_ref, k1_ref, k2_ref, vt_ref, lam_ref, g_ref, o_ref,
                 s1a_ref, s2a_ref, s1b_ref, s2b_ref, bias_ref,
                 m1_ref, acc1_ref, m2_ref, acc2_ref, *, tk, lam_init):
    h = pl.program_id(1)
    i = pl.program_id(2)

    @pl.when(i == 0)
    def _():
        slope2 = jnp.exp2(jnp.full((1, 1), -2.0, F32) * (h + 1).astype(F32)) * (2.0 * LOG2E)
        bias_ref[...] = _diag_bias(tk, True, slope2)

    maps = (_Map(q1_ref, q1n_ref, k1_ref, vt_ref, bias_ref, m1_ref, acc1_ref),
            _Map(q2_ref, q2n_ref, k2_ref, vt_ref, bias_ref, m2_ref, acc2_ref))
    _flash(i, maps, (s1a_ref, s2a_ref), (s1b_ref, s2b_ref), tk)

    lams = lam_ref[...]
    lam = (jnp.exp(jnp.sum(lams[0:1] * lams[1:2], axis=-1, keepdims=True))
           - jnp.exp(jnp.sum(lams[2:3] * lams[3:4], axis=-1, keepdims=True)) + lam_init)
    o_t = _normalized(acc1_ref, HEAD_PAD) - lam * _normalized(acc2_ref, HEAD_PAD)
    o_ref[0] = (_rms(o_t.T, g_ref[...]) * (1.0 - lam_init)).astype(BF16)


def _attn_specs(s, v_rows):
    tq, tk = TILE_Q, TILE_K
    last = s // tq - 1
    q_spec = pl.BlockSpec((1, 1, tq, HEAD_PAD), lambda b, h, i: (b, h, i, 0))
    q_next_spec = pl.BlockSpec((1, 1, tq, HEAD_PAD),
                               lambda b, h, i: (b, h, jnp.minimum(i + 1, last), 0))
    k_spec = pl.BlockSpec((1, 1, s, HEAD_PAD), lambda b, h, i: (b, h, 0, 0))
    vt_spec = pl.BlockSpec((1, 1, s // tk, v_rows, tk), lambda b, h, i: (b, h, 0, 0, 0))
    o_spec = pl.BlockSpec((1, tq, HEAD_PAD), lambda b, h, i: (b, i, h))
    return (q_spec, q_next_spec), k_spec, vt_spec, o_spec


def _fox_mla_attention(fq, fk, fvt, mq, mk, mvt):
    b, _, s, _ = fq.shape
    tq, tk = TILE_Q, TILE_K
    q_spec, k_spec, vt_spec, o_spec = _attn_specs(s, V_ROWS_ONES)
    score = pltpu.VMEM((tk, tq), F32)
    bias = pltpu.VMEM((tk, tk), F32)
    stat = pltpu.VMEM((1, tq), F32)
    acc = pltpu.VMEM((V_ROWS_ONES, tq), F32)
    return pl.pallas_call(
        functools.partial(_fox_mla_kernel, tk=tk),
        grid=(b, HEADS, s // tq),
        in_specs=[*q_spec, k_spec, vt_spec, *q_spec, k_spec, vt_spec],
        out_specs=o_spec,
        out_shape=jax.ShapeDtypeStruct((b, s, _GROUP), BF16),
        scratch_shapes=[score, score, score, score, bias, bias, stat, acc, stat, acc],
        compiler_params=pltpu.CompilerParams(
            dimension_semantics=("arbitrary",) * 3, vmem_limit_bytes=VMEM_LIMIT),
        name="fox_mla_attn",
    )(fq, fq, fk, fvt, mq, mq, mk, mvt)


def _diff_attention(q1, q2, k1, k2, vt, layer, lams, g, lam_init):
    b, _, s, _ = q1.shape
    tq, tk = TILE_Q, TILE_K
    q_spec, k_spec, vt_spec, o_spec = _attn_specs(s, V_ROWS_DIFF)
    score = pltpu.VMEM((tk, tq), F32)
    stat = pltpu.VMEM((1, tq), F32)
    acc = pltpu.VMEM((V_ROWS_DIFF, tq), F32)
    return pl.pallas_call(
        functools.partial(_diff_kernel, tk=tk, lam_init=lam_init),
        grid=(b, HEADS, s // tq),
        in_specs=[*q_spec, *q_spec, k_spec, k_spec, vt_spec,
                  _layer_spec((4, QK_DIM), layer), _layer_spec((1, HEAD_PAD), layer)],
        out_specs=o_spec,
        out_shape=jax.ShapeDtypeStruct((b, s, _GROUP), BF16),
        scratch_shapes=[score, score, score, score, pltpu.VMEM((tk, tk), F32),
                        stat, acc, stat, acc],
        compiler_params=pltpu.CompilerParams(
            dimension_semantics=("arbitrary",) * 3, vmem_limit_bytes=VMEM_LIMIT),
        name="diff_attn",
    )(q1, q1, q2, q2, k1, k2, vt, lams, g)


def _ffn_kernel(x_ref, ofm_ref, od_ref, wofm_ref, wod_ref, g2_ref,
                wup_ref, cw_ref, cb_ref, wdn_ref, gf_ref, o_ref, xn_ref, act_ref,
                *, tm, final):
    i = pl.program_id(1)

    @pl.when(i == 0)
    def _():
        xn_ref[0:HALO, :] = jnp.zeros((HALO, D_MODEL), BF16)

    @pl.when(i > 0)
    def _():
        xn_ref[0:HALO, :] = xn_ref[tm:tm + HALO, :]

    x1 = x_ref[0] + _dot(ofm_ref[0], wofm_ref[...]) + _dot(od_ref[0], wod_ref[...])
    xn_ref[HALO:HALO + tm, :] = _rms(x1, g2_ref[...]).astype(BF16)

    def conv(up, off):
        cw = cw_ref[:, off:off + FF_CHUNK]
        y = cb_ref[:, off:off + FF_CHUNK]
        for j in range(CONV_W):
            back = CONV_W - 1 - j
            tap = up if back == 0 else pltpu.roll(up, back, axis=0)
            y = y + tap[HALO:HALO + tm, :] * cw[j:j + 1, :]
        return y

    xn = xn_ref[...]
    for c in range(D_FF // FF_CHUNK):
        og = c * FF_CHUNK
        ou = D_FF + og
        yg = conv(_dot(xn, wup_ref[:, og:og + FF_CHUNK]), og)
        yu = conv(_dot(xn, wup_ref[:, ou:ou + FF_CHUNK]), ou)
        act = yg * (1.0 / (1.0 + jnp.exp(-yg))) * yu
        act_ref[:, og:og + FF_CHUNK] = act.astype(BF16)

    x2 = x1 + _dot(act_ref[...], wdn_ref[...])
    if final:
        x2 = _rms(x2, gf_ref[...])
    o_ref[0] = x2


def _ffn(x, o_fm, o_diff, layer, wo_fm, wo_d, g2, w_up, conv_w, conv_b, w_down, g_final, final):
    b, s, _ = x.shape
    per_layer = functools.partial(_layer_spec, layer=layer)
    tm = TILE_FFN
    row = lambda width: pl.BlockSpec((1, tm, width), lambda bi, i: (bi, i, 0))
    return pl.pallas_call(
        functools.partial(_ffn_kernel, tm=tm, final=final),
        grid=(b, s // tm),
        in_specs=[
            row(D_MODEL), row(_GROUP), row(_GROUP),
            per_layer((_GROUP, D_MODEL)), per_layer((_GROUP, D_MODEL)),
            per_layer((1, D_MODEL)),
            per_layer((D_MODEL, 2 * D_FF)),
            per_layer((CONV_W, 2 * D_FF)),
            per_layer((1, 2 * D_FF)),
            per_layer((D_FF, D_MODEL)),
            _const_spec((1, D_MODEL)),
        ],
        out_specs=row(D_MODEL),
        out_shape=jax.ShapeDtypeStruct((b, s, D_MODEL), F32),
        scratch_shapes=[pltpu.VMEM((HALO + tm, D_MODEL), BF16), pltpu.VMEM((tm, D_FF), BF16)],
        compiler_params=pltpu.CompilerParams(
            dimension_semantics=("arbitrary", "arbitrary"), vmem_limit_bytes=VMEM_LIMIT),
        name="ffn",
    )(x, o_fm, o_diff, wo_fm, wo_d, g2, w_up, conv_w, conv_b, w_down, g_final)


def _rot_half_cols(w):
    half = MLA_ROPE // 2
    return jnp.concatenate([w[..., half:], w[..., :half]], axis=-1)


def _layout_w_in(w):
    n_ff = OFF_FF + HEADS
    n_kr = w.shape[-1] - MLA_ROPE
    zeros = lambda n: jnp.zeros(w.shape[:-1] + (n,), w.dtype)
    kr = w[..., n_kr:]
    rope_pad = HEAD_PAD - QK_DIM - MLA_ROPE
    out = jnp.concatenate(
        [w[..., :n_ff], zeros(128 - HEADS), w[..., n_ff:n_kr],
         zeros(QK_DIM), kr, zeros(rope_pad),
         zeros(QK_DIM), _rot_half_cols(kr), zeros(rope_pad)], axis=-1).astype(BF16)
    assert out.shape[-1] == PROJ_COLS
    return out


def _layout_w_uq(w):
    w = w.reshape(DEPTH, MLA_Q_RANK, HEADS, QK_DIM + MLA_ROPE)
    nope, rope = w[..., :QK_DIM], w[..., QK_DIM:]
    pad = jnp.zeros(w.shape[:-1] + (HEAD_PAD - QK_DIM - MLA_ROPE,), w.dtype)
    direct = jnp.concatenate([nope, rope, pad], axis=-1)
    swapped = jnp.concatenate([jnp.zeros_like(nope), _rot_half_cols(rope), pad], axis=-1)
    return jnp.concatenate([direct.reshape(DEPTH, MLA_Q_RANK, -1),
                            swapped.reshape(DEPTH, MLA_Q_RANK, -1)], axis=-1).astype(BF16)


def _layout_w_ukv(w):
    w = w.reshape(DEPTH, MLA_KV_RANK, HEADS, 2 * QK_DIM)
    pad = jnp.zeros(w.shape[:-1] + (HEAD_PAD - QK_DIM,), w.dtype)
    k = jnp.concatenate([w[..., :QK_DIM], pad], axis=-1).reshape(DEPTH, MLA_KV_RANK, -1)
    v = jnp.concatenate([w[..., QK_DIM:], pad], axis=-1).reshape(DEPTH, MLA_KV_RANK, -1)
    return jnp.concatenate([k, v], axis=-1).astype(BF16)


def _interleave_head_rows(w_fox, w_mla):
    parts = [w.reshape(DEPTH, HEADS, QK_DIM, -1) for w in (w_fox, w_mla)]
    return jnp.concatenate(parts, axis=2).reshape(DEPTH, HEADS * HEAD_PAD, -1)


def _rope_tables(s):
    half = MLA_ROPE // 2
    inv = 1.0 / (ROPE_THETA ** (jnp.arange(half, dtype=F32) / half))
    pos = (jnp.arange(s * half, dtype=jnp.int32) // half).astype(F32)
    ang = pos * jnp.tile(inv, s)
    cos, sin = jnp.cos(ang).reshape(s, half), jnp.sin(ang).reshape(s, half)
    tail = jnp.zeros((s, HEAD_PAD - QK_DIM - MLA_ROPE), F32)
    cos_t = jnp.concatenate([jnp.ones((s, QK_DIM), F32), cos, cos, tail], axis=1)
    sin_t = jnp.concatenate([jnp.zeros((s, QK_DIM), F32), -sin, sin, tail], axis=1)
    return cos_t, sin_t


def kernel(x, ln1_g, w_in, fgate_b, lam_q1, lam_k1, lam_q2, lam_k2, diff_norm_g, q_norm_g, w_uq,
           kv_norm_g, w_ukv, w_o, ln2_g, w_up, conv_w, conv_b, w_down, final_g):
    s = x.shape[1]
    cos_t, sin_t = _rope_tables(s)
    row = lambda v: v.reshape(DEPTH, 1, -1)
    wa = _layout_w_in(w_in)
    fb = row(jnp.pad(fgate_b, ((0, 0), (0, 128 - HEADS))))
    wuq = _layout_w_uq(w_uq)
    wukv = _layout_w_ukv(w_ukv)
    lams = jnp.stack([lam_q1, lam_k1, lam_q2, lam_k2], axis=1)
    wo = w_o.astype(BF16)
    wo_fm = _interleave_head_rows(wo[:, :_FOX], wo[:, _FOX + _GROUP:])
    wo_d = wo[:, _FOX:_FOX + _GROUP]
    wup = w_up.astype(BF16)
    wdn = w_down.astype(BF16)
    g1, g2, qng, kvng, dng, cb = (row(v) for v in (ln1_g, ln2_g, q_norm_g, kv_norm_g,
                                                   diff_norm_g, conv_b))
    for i in range(DEPTH):
        lam_init = 0.8 - 0.6 * math.exp(-0.3 * i)
        heads = _project(x, i, g1, wa, fb, qng, wuq, kvng, wukv, cos_t, sin_t)
        fq, fk, fvt, dq1, dq2, dk1, dk2, dvt, mq, mk, mvt = heads
        o_fm = _fox_mla_attention(fq, fk, fvt, mq, mk, mvt)
        o_diff = _diff_attention(dq1, dq2, dk1, dk2, dvt, i, lams, dng, lam_init)
        x = _ffn(x, o_fm, o_diff, i, wo_fm, wo_d, g2, wup, conv_w, cb, wdn,
                 final_g.reshape(1, -1), final=(i == DEPTH - 1))
    return x
```

```python
import functools
import math
from typing import NamedTuple

import jax
import jax.numpy as jnp
from jax import lax
from jax.experimental import pallas as pl
from jax.experimental.pallas import tpu as pltpu

F32 = jnp.float32
BF16 = jnp.bfloat16

D_MODEL = 1024
DEPTH = 2
CHUNK = 64
EPS = 1e-6
MASKED = -1e30

HEADS = 4
HEAD_PAD = 128
QK_DIM = 64
MLA_ROPE = 32
MLA_Q_RANK = 256
MLA_KV_RANK = 128
ROPE_THETA = 10000.0
D_FF = 2816
CONV_W = 3

LOG2E = 1.4426950408889634

_GROUP = HEADS * HEAD_PAD
_FOX = HEADS * QK_DIM
OFF_FQ, OFF_FK, OFF_FV = 0, _FOX, 2 * _FOX
COLS_FOX = 3 * _FOX
OFF_DQ = 0
OFF_DK = OFF_DQ + _GROUP
OFF_DV = OFF_DK + _GROUP
OFF_CQ = OFF_DV + _GROUP
OFF_CKV = OFF_CQ + MLA_Q_RANK
COLS_MAIN = OFF_CKV + MLA_KV_RANK
OFF_FF, OFF_KR, OFF_KRP = 0, 128, 256
COLS_SMALL = 384

TILE_K = 512
TILE_Q = 2 * TILE_K
TILE_PROJ = TILE_K
BF16_ROWS = 16
V_ROWS_ONES = QK_DIM + BF16_ROWS
V_ROWS_DIFF = HEAD_PAD + BF16_ROWS
TILE_FFN = 512
FF_CHUNK = 256
HALO = 16
VMEM_LIMIT = 56 * 1024 * 1024


def _dot(a, b):
    return jnp.dot(a, b, preferred_element_type=F32)


def _dot_nt(a, b):
    return lax.dot_general(a, b, (((1,), (1,)), ((), ())), preferred_element_type=F32)


def _rms(x, g):
    return x * lax.rsqrt(jnp.mean(x * x, axis=-1, keepdims=True) + EPS) * g


def _split3(x):
    hi = x.astype(BF16)
    r1 = x - hi.astype(F32)
    mid = r1.astype(BF16)
    lo = (r1 - mid.astype(F32)).astype(BF16)
    return hi, mid, lo


def _const_spec(shape):
    nd = len(shape)
    return pl.BlockSpec(shape, lambda *_: (0,) * nd, pipeline_mode=pl.Buffered(1))


def _layer_spec(shape, layer):
    nd = len(shape)
    return pl.BlockSpec((None,) + tuple(shape), lambda *_: (layer,) + (0,) * nd,
                        pipeline_mode=pl.Buffered(1))


def _proj_kernel(x_ref, g1_ref, wfox_ref, wmain_ref, wsmall_ref, fb_ref, qng_ref, wuq_ref,
                 kvng_ref, wukv_ref,
                 cos_ref, sin_ref,
                 fq_ref, fk_ref, fvt_ref, dq1_ref, dq2_ref, dk1_ref, dk2_ref, dvt_ref,
                 mq_ref, mk_ref, mvt_ref, carry_ref, *, tm):
    i = pl.program_id(1)

    @pl.when(i == 0)
    def _():
        carry_ref[...] = jnp.zeros_like(carry_ref)

    xn = _rms(x_ref[0], g1_ref[...]).astype(BF16)

    def proj(w_ref, off, width):
        return _dot(xn, w_ref[:, off:off + width])

    def head(t, h):
        return t[:, h * HEAD_PAD:(h + 1) * HEAD_PAD]

    lane = lax.broadcasted_iota(jnp.int32, (tm, HEAD_PAD), 1)
    low = lane < QK_DIM
    extra = jnp.where(low, lane, lane - QK_DIM)
    in_half = (low, jnp.logical_not(low))
    ones3 = tuple(jnp.where(jnp.logical_not(m) & (extra < 3), 1.0, 0.0) for m in in_half)
    one1 = tuple(jnp.where(jnp.logical_not(m) & (extra == 0), 1.0, 0.0) for m in in_half)
    qk_scale = QK_DIM ** -0.5 * LOG2E

    def v_rows(t, half):
        t = t.T
        if half == 0:
            return t[:V_ROWS_ONES]
        return jnp.concatenate([t[QK_DIM:], t[:V_ROWS_ONES - QK_DIM]], axis=0)

    z = proj(wsmall_ref, OFF_FF, 128) + fb_ref[...]
    logf = jnp.minimum(z, 0.0) - jnp.log1p(jnp.exp(-jnp.abs(z)))
    rr = lax.broadcasted_iota(jnp.int32, (tm, tm), 0)
    cc = lax.broadcasted_iota(jnp.int32, (tm, tm), 1)
    tri = (cc <= rr).astype(BF16)
    parts = _dot(tri, jnp.concatenate(_split3(logf), axis=1))
    cum = parts[:, :128] + parts[:, 128:256] + parts[:, 256:] + carry_ref[...]
    carry_ref[...] = cum[tm - 1:tm, :]
    pr = lax.broadcasted_iota(jnp.int32, (3 * 128, _GROUP), 0)
    pc = lax.broadcasted_iota(jnp.int32, (3 * 128, _GROUP), 1)
    term, hd = pr // 128, pr % 128
    place = pc - hd * HEAD_PAD - jnp.where(hd % 2 == 0, QK_DIM, 0) == term
    fk_bias = _dot(jnp.concatenate(_split3(cum * (-LOG2E)), axis=1), place.astype(BF16))
    hq = proj(wfox_ref, OFF_FQ, _FOX)
    hk = proj(wfox_ref, OFF_FK, _FOX)
    hv = proj(wfox_ref, OFF_FV, _FOX)
    for h in range(HEADS):
        pair, half = divmod(h, 2)
        keep = in_half[half]
        fq_ref[0, h] = jnp.where(keep, head(hq, pair) * qk_scale, ones3[half]).astype(BF16)
        fk_ref[0, h] = jnp.where(keep, head(hk, pair), head(fk_bias, h)).astype(BF16)
        fvt_ref[0, h, 0] = v_rows(jnp.where(keep, head(hv, pair), one1[half]), half).astype(BF16)

    pos = (i * tm + lax.broadcasted_iota(jnp.int32, (tm, HEAD_PAD), 0)).astype(F32)
    hq = proj(wmain_ref, OFF_DQ, _GROUP)
    hk = proj(wmain_ref, OFF_DK, _GROUP)
    hv = proj(wmain_ref, OFF_DV, _GROUP)
    ones_rows = jnp.where(lax.broadcasted_iota(jnp.int32, (BF16_ROWS, tm), 0) == 0, 1.0, 0.0)
    for h in range(HEADS):
        slope = 2.0 ** (-8.0 * (h + 1) / HEADS)
        a_hi, a_mid, a_lo = _split3(pos * (slope * LOG2E))
        kbias = jnp.where(extra == 0, a_hi.astype(F32),
                          jnp.where(extra == 1, a_mid.astype(F32),
                                    jnp.where(extra == 2, a_lo.astype(F32), 0.0)))
        q = head(hq, h) * qk_scale
        k = head(hk, h)
        for half, (q_ref, k_ref) in enumerate(((dq1_ref, dk1_ref), (dq2_ref, dk2_ref))):
            q_ref[0, h] = jnp.where(in_half[half], q, ones3[half]).astype(BF16)
            k_ref[0, h] = jnp.where(in_half[half], k, kbias).astype(BF16)
        dvt_ref[0, h, 0] = jnp.concatenate([head(hv, h).T, ones_rows], axis=0).astype(BF16)

    cqn = _rms(proj(wmain_ref, OFF_CQ, MLA_Q_RANK), qng_ref[...]).astype(BF16)
    ckvn = _rms(proj(wmain_ref, OFF_CKV, MLA_KV_RANK), kvng_ref[...]).astype(BF16)
    cos = cos_ref[...]
    sin = sin_ref[...]
    krot = proj(wsmall_ref, OFF_KR, 128) * cos + proj(wsmall_ref, OFF_KRP, 128) * sin
    qa = _dot(cqn, wuq_ref[:, :_GROUP])
    qb = _dot(cqn, wuq_ref[:, _GROUP:])
    kk = _dot(ckvn, wukv_ref[:, :_GROUP])
    vv = _dot(ckvn, wukv_ref[:, _GROUP:])
    mla_scale = (QK_DIM + MLA_ROPE) ** -0.5 * LOG2E
    for h in range(HEADS):
        mq_ref[0, h] = ((head(qa, h) * cos + head(qb, h) * sin) * mla_scale).astype(BF16)
        mk_ref[0, h] = (head(kk, h) + krot).astype(BF16)
        mvt_ref[0, h, 0] = v_rows(head(vv, h) + one1[0], 0).astype(BF16)


def _project(x, layer, g1, w_groups, fb, qng, wuq, kvng, wukv, cos_t, sin_t):
    b, s, _ = x.shape
    per_layer = functools.partial(_layer_spec, layer=layer)
    tm = TILE_PROJ
    rows = (jax.ShapeDtypeStruct((b, HEADS, s, HEAD_PAD), BF16),
            pl.BlockSpec((1, HEADS, tm, HEAD_PAD), lambda bi, i: (bi, 0, i, 0)))
    def cols(v_rows):
        return (jax.ShapeDtypeStruct((b, HEADS, s // tm, v_rows, tm), BF16),
                pl.BlockSpec((1, HEADS, 1, v_rows, tm), lambda bi, i: (bi, 0, i, 0, 0)))

    outs = [rows, rows, cols(V_ROWS_ONES), rows, rows, rows, rows, cols(V_ROWS_DIFF),
            rows, rows, cols(V_ROWS_ONES)]
    row_tab = pl.BlockSpec((tm, HEAD_PAD), lambda bi, i: (i, 0))
    return pl.pallas_call(
        functools.partial(_proj_kernel, tm=tm),
        grid=(b, s // tm),
        in_specs=[
            pl.BlockSpec((1, tm, D_MODEL), lambda bi, i: (bi, i, 0)),
            per_layer((1, D_MODEL)),
            per_layer((D_MODEL, COLS_FOX)),
            per_layer((D_MODEL, COLS_MAIN)),
            per_layer((D_MODEL, COLS_SMALL)),
            per_layer((1, 128)),
            per_layer((1, MLA_Q_RANK)),
            per_layer((MLA_Q_RANK, 2 * _GROUP)),
            per_layer((1, MLA_KV_RANK)),
            per_layer((MLA_KV_RANK, 2 * _GROUP)),
            row_tab, row_tab,
        ],
        out_specs=[o[1] for o in outs],
        out_shape=[o[0] for o in outs],
        scratch_shapes=[pltpu.VMEM((1, 128), F32)],
        compiler_params=pltpu.CompilerParams(
            dimension_semantics=("arbitrary", "arbitrary"), vmem_limit_bytes=VMEM_LIMIT),
        name="proj",
    )(x, g1, *w_groups, fb, qng, wuq, kvng, wukv, cos_t, sin_t)


def _tile_ids(t):
    key = lax.broadcasted_iota(jnp.int32, (t, t), 0)
    qry = lax.broadcasted_iota(jnp.int32, (t, t), 1)
    return key, qry


def _online_step(s, vt, m_ref, acc_ref, lanes=slice(None)):
    m_prev = m_ref[:, lanes]
    m_new = jnp.maximum(m_prev, jnp.max(s, axis=0, keepdims=True))
    p = jnp.exp2(s - m_new)
    alpha = jnp.exp2(m_prev - m_new)
    acc_ref[:, lanes] = alpha * acc_ref[:, lanes] + _dot(vt, p.astype(BF16))
    m_ref[:, lanes] = m_new


class _Map(NamedTuple):
    q_ref: object
    q_next_ref: object
    k_ref: object
    vt_ref: object
    bias_ref: object
    m_ref: object
    acc_ref: object


def _flash(i, maps, buf_a, buf_b, tk):
    for mp in maps:
        mp.m_ref[...] = jnp.full_like(mp.m_ref, MASKED)
        mp.acc_ref[...] = jnp.zeros_like(mp.acc_ref)

    def k_tile(mp, kt):
        return mp.k_ref[0, 0, pl.ds(pl.multiple_of(kt * tk, tk), tk), :]

    def compute_scores(kt, buf):
        for mp, s_ref in zip(maps, buf):
            s_ref[...] = _dot_nt(k_tile(mp, kt), mp.q_ref[0, 0])

    def consume(kt, buf):
        for mp, s_ref in zip(maps, buf):
            _online_step(s_ref[...], mp.vt_ref[0, 0, kt], mp.m_ref, mp.acc_ref)

    @pl.when(i == 0)
    def _():
        compute_scores(0, buf_a)

    def pair(j):
        compute_scores(j + 1, buf_b)
        consume(j, buf_a)
        compute_scores(j + 2, buf_a)
        consume(j + 1, buf_b)

    odd = jnp.bitwise_and(i, 1)

    @pl.when(odd == 1)
    def _():
        pair(0)

    def body(jj, c):
        j = 2 * odd + 4 * jj
        pair(j)
        pair(j + 2)
        return c

    lax.fori_loop(0, lax.shift_right_logical(i, 1), body, 0)

    lower, upper = slice(0, tk), slice(tk, None)
    for mp, sb_ref in zip(maps, buf_b):
        sb_ref[:, lower] = _dot_nt(k_tile(mp, 2 * i + 1), mp.q_ref[0, 0, upper, :])
    for mp, sa_ref, sb_ref in zip(maps, buf_a, buf_b):
        bias = mp.bias_ref[...]
        vt0, vt1 = mp.vt_ref[0, 0, 2 * i], mp.vt_ref[0, 0, 2 * i + 1]
        _online_step(sa_ref[:, lower] + bias, vt0, mp.m_ref, mp.acc_ref, lanes=lower)
        s = jnp.concatenate([sa_ref[:, upper], sb_ref[:, lower] + bias], axis=0)
        _online_step(s, jnp.concatenate([vt0, vt1], axis=1), mp.m_ref, mp.acc_ref, lanes=upper)

    for mp, sa_ref in zip(maps, buf_a):
        sa_ref[...] = _dot_nt(k_tile(mp, 0), mp.q_next_ref[0, 0])


def _normalized(acc_ref, rows):
    acc = acc_ref[...]
    return acc[:rows] * (1.0 / acc[rows:rows + 1, :])


def _diag_bias(tk, chunked, slope2=None):
    key, qry = _tile_ids(tk)
    keep = (key // CHUNK) <= (qry // CHUNK) if chunked else key <= qry
    if slope2 is None:
        return jnp.where(keep, 0.0, MASKED)
    return jnp.where(keep, jnp.maximum(key - qry, 0).astype(F32) * -slope2, MASKED)


def _fox_mla_kernel(fq_ref, fqn_ref, fk_ref, fvt_ref, mq_ref, mqn_ref, mk_ref, mvt_ref, o_ref,
                    sfa_ref, sma_ref, sfb_ref, smb_ref, fbias_ref, mbias_ref,
                    fm_ref, facc_ref, mm_ref, macc_ref, *, tk):
    i = pl.program_id(2)

    @pl.when(i == 0)
    def _():
        fbias_ref[...] = _diag_bias(tk, False)
        mbias_ref[...] = _diag_bias(tk, True)

    maps = (_Map(fq_ref, fqn_ref, fk_ref, fvt_ref, fbias_ref, fm_ref, facc_ref),
            _Map(mq_ref, mqn_ref, mk_ref, mvt_ref, mbias_ref, mm_ref, macc_ref))
    _flash(i, maps, (sfa_ref, sma_ref), (sfb_ref, smb_ref), tk)

    o_t = jnp.concatenate([_normalized(mp.acc_ref, QK_DIM) for mp in maps], axis=0)
    o_ref[0] = o_t.T.astype(BF16)


def _diff_kernel(q1_ref, q1n_ref, q2_ref, q2n_ref, k1_ref, k2_ref, vt_ref, lam_ref, g_ref, o_ref,
                 s1a_ref, s2a_ref, s1b_ref, s2b_ref, bias_ref,
                 m1_ref, acc1_ref, m2_ref, acc2_ref, *, tk, lam_init):
    h = pl.program_id(1)
    i = pl.program_id(2)

    @pl.when(i == 0)
    def _():
        slope2 = jnp.exp2(jnp.full((1, 1), -2.0, F32) * (h + 1).astype(F32)) * (2.0 * LOG2E)
        bias_ref[...] = _diag_bias(tk, True, slope2)

    maps = (_Map(q1_ref, q1n_ref, k1_ref, vt_ref, bias_ref, m1_ref, acc1_ref),
            _Map(q2_ref, q2n_ref, k2_ref, vt_ref, bias_ref, m2_ref, acc2_ref))
    _flash(i, maps, (s1a_ref, s2a_ref), (s1b_ref, s2b_ref), tk)

    lams = lam_ref[...]
    lam = (jnp.exp(jnp.sum(lams[0:1] * lams[1:2], axis=-1, keepdims=True))
           - jnp.exp(jnp.sum(lams[2:3] * lams[3:4], axis=-1, keepdims=True)) + lam_init)
    o_t = _normalized(acc1_ref, HEAD_PAD) - lam * _normalized(acc2_ref, HEAD_PAD)
    o_ref[0] = (_rms(o_t.T, g_ref[...]) * (1.0 - lam_init)).astype(BF16)


def _attn_specs(s, v_rows):
    tq, tk = TILE_Q, TILE_K
    last = s // tq - 1
    q_spec = pl.BlockSpec((1, 1, tq, HEAD_PAD), lambda b, h, i: (b, h, i, 0))
    q_next_spec = pl.BlockSpec((1, 1, tq, HEAD_PAD),
                               lambda b, h, i: (b, h, jnp.minimum(i + 1, last), 0))
    k_spec = pl.BlockSpec((1, 1, s, HEAD_PAD), lambda b, h, i: (b, h, 0, 0))
    vt_spec = pl.BlockSpec((1, 1, s // tk, v_rows, tk), lambda b, h, i: (b, h, 0, 0, 0))
    o_spec = pl.BlockSpec((1, tq, HEAD_PAD), lambda b, h, i: (b, i, h))
    return (q_spec, q_next_spec), k_spec, vt_spec, o_spec


def _fox_mla_attention(fq, fk, fvt, mq, mk, mvt):
    b, _, s, _ = fq.shape
    tq, tk = TILE_Q, TILE_K
    q_spec, k_spec, vt_spec, o_spec = _attn_specs(s, V_ROWS_ONES)
    score = pltpu.VMEM((tk, tq), F32)
    bias = pltpu.VMEM((tk, tk), F32)
    stat = pltpu.VMEM((1, tq), F32)
    acc = pltpu.VMEM((V_ROWS_ONES, tq), F32)
    return pl.pallas_call(
        functools.partial(_fox_mla_kernel, tk=tk),
        grid=(b, HEADS, s // tq),
        in_specs=[*q_spec, k_spec, vt_spec, *q_spec, k_spec, vt_spec],
        out_specs=o_spec,
        out_shape=jax.ShapeDtypeStruct((b, s, _GROUP), BF16),
        scratch_shapes=[score, score, score, score, bias, bias, stat, acc, stat, acc],
        compiler_params=pltpu.CompilerParams(
            dimension_semantics=("arbitrary",) * 3, vmem_limit_bytes=VMEM_LIMIT),
        name="fox_mla_attn",
    )(fq, fq, fk, fvt, mq, mq, mk, mvt)


def _diff_attention(q1, q2, k1, k2, vt, layer, lams, g, lam_init):
    b, _, s, _ = q1.shape
    tq, tk = TILE_Q, TILE_K
    q_spec, k_spec, vt_spec, o_spec = _attn_specs(s, V_ROWS_DIFF)
    score = pltpu.VMEM((tk, tq), F32)
    stat = pltpu.VMEM((1, tq), F32)
    acc = pltpu.VMEM((V_ROWS_DIFF, tq), F32)
    return pl.pallas_call(
        functools.partial(_diff_kernel, tk=tk, lam_init=lam_init),
        grid=(b, HEADS, s // tq),
        in_specs=[*q_spec, *q_spec, k_spec, k_spec, vt_spec,
                  _layer_spec((4, QK_DIM), layer), _layer_spec((1, HEAD_PAD), layer)],
        out_specs=o_spec,
        out_shape=jax.ShapeDtypeStruct((b, s, _GROUP), BF16),
        scratch_shapes=[score, score, score, score, pltpu.VMEM((tk, tk), F32),
                        stat, acc, stat, acc],
        compiler_params=pltpu.CompilerParams(
            dimension_semantics=("arbitrary",) * 3, vmem_limit_bytes=VMEM_LIMIT),
        name="diff_attn",
    )(q1, q1, q2, q2, k1, k2, vt, lams, g)


def _ffn_kernel(x_ref, ofm_ref, od_ref, wofm_ref, wod_ref, g2_ref,
                wup_ref, cw_ref, cb_ref, wdn_ref, gf_ref, o_ref, xn_ref, act_ref,
                *, tm, final):
    i = pl.program_id(1)

    @pl.when(i == 0)
    def _():
        xn_ref[0:HALO, :] = jnp.zeros((HALO, D_MODEL), BF16)

    @pl.when(i > 0)
    def _():
        xn_ref[0:HALO, :] = xn_ref[tm:tm + HALO, :]

    x1 = x_ref[0] + _dot(ofm_ref[0], wofm_ref[...]) + _dot(od_ref[0], wod_ref[...])
    xn_ref[HALO:HALO + tm, :] = _rms(x1, g2_ref[...]).astype(BF16)

    def conv(up, off):
        cw = cw_ref[:, off:off + FF_CHUNK]
        y = cb_ref[:, off:off + FF_CHUNK]
        for j in range(CONV_W):
            back = CONV_W - 1 - j
            tap = up if back == 0 else pltpu.roll(up, back, axis=0)
            y = y + tap[HALO:HALO + tm, :] * cw[j:j + 1, :]
        return y

    xn = xn_ref[...]
    for c in range(D_FF // FF_CHUNK):
        og = c * FF_CHUNK
        ou = D_FF + og
        yg = conv(_dot(xn, wup_ref[:, og:og + FF_CHUNK]), og)
        yu = conv(_dot(xn, wup_ref[:, ou:ou + FF_CHUNK]), ou)
        act = yg * (1.0 / (1.0 + jnp.exp(-yg))) * yu
        act_ref[:, og:og + FF_CHUNK] = act.astype(BF16)

    x2 = x1 + _dot(act_ref[...], wdn_ref[...])
    if final:
        x2 = _rms(x2, gf_ref[...])
    o_ref[0] = x2


def _ffn(x, o_fm, o_diff, layer, wo_fm, wo_d, g2, w_up, conv_w, conv_b, w_down, g_final, final):
    b, s, _ = x.shape
    per_layer = functools.partial(_layer_spec, layer=layer)
    tm = TILE_FFN
    row = lambda width: pl.BlockSpec((1, tm, width), lambda bi, i: (bi, i, 0))
    return pl.pallas_call(
        functools.partial(_ffn_kernel, tm=tm, final=final),
        grid=(b, s // tm),
        in_specs=[
            row(D_MODEL), row(_GROUP), row(_GROUP),
            per_layer((_GROUP, D_MODEL)), per_layer((_GROUP, D_MODEL)),
            per_layer((1, D_MODEL)),
            per_layer((D_MODEL, 2 * D_FF)),
            per_layer((CONV_W, 2 * D_FF)),
            per_layer((1, 2 * D_FF)),
            per_layer((D_FF, D_MODEL)),
            _const_spec((1, D_MODEL)),
        ],
        out_specs=row(D_MODEL),
        out_shape=jax.ShapeDtypeStruct((b, s, D_MODEL), F32),
        scratch_shapes=[pltpu.VMEM((HALO + tm, D_MODEL), BF16), pltpu.VMEM((tm, D_FF), BF16)],
        compiler_params=pltpu.CompilerParams(
            dimension_semantics=("arbitrary", "arbitrary"), vmem_limit_bytes=VMEM_LIMIT),
        name="ffn",
    )(x, o_fm, o_diff, wo_fm, wo_d, g2, w_up, conv_w, conv_b, w_down, g_final)


def _rot_half_cols(w):
    half = MLA_ROPE // 2
    return jnp.concatenate([w[..., half:], w[..., :half]], axis=-1)


def _split_w_in(w):
    n_main = COLS_FOX + HEADS
    n_kr = w.shape[-1] - MLA_ROPE
    zeros = lambda n: jnp.zeros(w.shape[:-1] + (n,), w.dtype)
    ff, kr = w[..., COLS_FOX:n_main], w[..., n_kr:]
    rope_pad = HEAD_PAD - QK_DIM - MLA_ROPE
    small = jnp.concatenate(
        [ff, zeros(128 - HEADS),
         zeros(QK_DIM), kr, zeros(rope_pad),
         zeros(QK_DIM), _rot_half_cols(kr), zeros(rope_pad)], axis=-1)
    groups = (w[..., :COLS_FOX], w[..., n_main:n_kr], small)
    assert [g.shape[-1] for g in groups] == [COLS_FOX, COLS_MAIN, COLS_SMALL]
    return tuple(g.astype(BF16) for g in groups)


def _layout_w_uq(w):
    w = w.reshape(DEPTH, MLA_Q_RANK, HEADS, QK_DIM + MLA_ROPE)
    nope, rope = w[..., :QK_DIM], w[..., QK_DIM:]
    pad = jnp.zeros(w.shape[:-1] + (HEAD_PAD - QK_DIM - MLA_ROPE,), w.dtype)
    direct = jnp.concatenate([nope, rope, pad], axis=-1)
    swapped = jnp.concatenate([jnp.zeros_like(nope), _rot_half_cols(rope), pad], axis=-1)
    return jnp.concatenate([direct.reshape(DEPTH, MLA_Q_RANK, -1),
                            swapped.reshape(DEPTH, MLA_Q_RANK, -1)], axis=-1).astype(BF16)


def _layout_w_ukv(w):
    w = w.reshape(DEPTH, MLA_KV_RANK, HEADS, 2 * QK_DIM)
    pad = jnp.zeros(w.shape[:-1] + (HEAD_PAD - QK_DIM,), w.dtype)
    k = jnp.concatenate([w[..., :QK_DIM], pad], axis=-1).reshape(DEPTH, MLA_KV_RANK, -1)
    v = jnp.concatenate([w[..., QK_DIM:], pad], axis=-1).reshape(DEPTH, MLA_KV_RANK, -1)
    return jnp.concatenate([k, v], axis=-1).astype(BF16)


def _interleave_head_rows(w_fox, w_mla):
    parts = [w.reshape(DEPTH, HEADS, QK_DIM, -1) for w in (w_fox, w_mla)]
    return jnp.concatenate(parts, axis=2).reshape(DEPTH, HEADS * HEAD_PAD, -1)


def _rope_tables(s):
    half = MLA_ROPE // 2
    inv = 1.0 / (ROPE_THETA ** (jnp.arange(half, dtype=F32) / half))
    pos = (jnp.arange(s * half, dtype=jnp.int32) // half).astype(F32)
    ang = pos * jnp.tile(inv, s)
    cos, sin = jnp.cos(ang).reshape(s, half), jnp.sin(ang).reshape(s, half)
    tail = jnp.zeros((s, HEAD_PAD - QK_DIM - MLA_ROPE), F32)
    cos_t = jnp.concatenate([jnp.ones((s, QK_DIM), F32), cos, cos, tail], axis=1)
    sin_t = jnp.concatenate([jnp.zeros((s, QK_DIM), F32), -sin, sin, tail], axis=1)
    return cos_t, sin_t


def kernel(x, ln1_g, w_in, fgate_b, lam_q1, lam_k1, lam_q2, lam_k2, diff_norm_g, q_norm_g, w_uq,
           kv_norm_g, w_ukv, w_o, ln2_g, w_up, conv_w, conv_b, w_down, final_g):
    s = x.shape[1]
    cos_t, sin_t = _rope_tables(s)
    row = lambda v: v.reshape(DEPTH, 1, -1)
    w_groups = _split_w_in(w_in)
    fb = row(jnp.pad(fgate_b, ((0, 0), (0, 128 - HEADS))))
    wuq = _layout_w_uq(w_uq)
    wukv = _layout_w_ukv(w_ukv)
    lams = jnp.stack([lam_q1, lam_k1, lam_q2, lam_k2], axis=1)
    wo = w_o.astype(BF16)
    wo_fm = _interleave_head_rows(wo[:, :_FOX], wo[:, _FOX + _GROUP:])
    wo_d = wo[:, _FOX:_FOX + _GROUP]
    wup = w_up.astype(BF16)
    wdn = w_down.astype(BF16)
    g1, g2, qng, kvng, dng, cb = (row(v) for v in (ln1_g, ln2_g, q_norm_g, kv_norm_g,
                                                   diff_norm_g, conv_b))
    for i in range(DEPTH):
        lam_init = 0.8 - 0.6 * math.exp(-0.3 * i)
        heads = _project(x, i, g1, w_groups, fb, qng, wuq, kvng, wukv, cos_t, sin_t)
        fq, fk, fvt, dq1, dq2, dk1, dk2, dvt, mq, mk, mvt = heads
        o_fm = _fox_mla_attention(fq, fk, fvt, mq, mk, mvt)
        o_diff = _diff_attention(dq1, dq2, dk1, dk2, dvt, i, lams, dng, lam_init)
        x = _ffn(x, o_fm, o_diff, i, wo_fm, wo_d, g2, wup, conv_w, cb, wdn,
                 final_g.reshape(1, -1), final=(i == DEPTH - 1))
    return x
```

```python
import functools
import math
from typing import NamedTuple

import jax
import jax.numpy as jnp
from jax import lax
from jax.experimental import pallas as pl
from jax.experimental.pallas import tpu as pltpu

F32 = jnp.float32
BF16 = jnp.bfloat16

D_MODEL = 1024
DEPTH = 2
CHUNK = 64
EPS = 1e-6
MASKED = -1e30

LANES = 128
HEADS = 4
HEAD_PAD = LANES
QK_DIM = 64
MLA_ROPE = 32
MLA_Q_RANK = 256
MLA_KV_RANK = 128
ROPE_THETA = 10000.0
D_FF = 2816
CONV_W = 3

LOG2E = 1.4426950408889634

_GROUP = HEADS * HEAD_PAD
_FOX = HEADS * QK_DIM
OFF_FQ, OFF_FK, OFF_FV = 0, _FOX, 2 * _FOX
COLS_FOX = 3 * _FOX
OFF_DQ = 0
OFF_DK = OFF_DQ + _GROUP
OFF_DV = OFF_DK + _GROUP
OFF_CQ = OFF_DV + _GROUP
OFF_CKV = OFF_CQ + MLA_Q_RANK
COLS_MAIN = OFF_CKV + MLA_KV_RANK
OFF_FF, OFF_KR, OFF_KRP = 0, LANES, 2 * LANES
COLS_SMALL = 3 * LANES

TILE_K = 512
TILE_Q = 2 * TILE_K
TILE_PROJ = TILE_K
BF16_ROWS = 16
V_ROWS_ONES = QK_DIM + BF16_ROWS
V_ROWS_DIFF = HEAD_PAD + BF16_ROWS
TILE_FFN = 512
FF_CHUNK = 256
HALO = 16
VMEM_LIMIT = 56 * 1024 * 1024


def _dot(a, b):
    return jnp.dot(a, b, preferred_element_type=F32)


def _dot_nt(a, b):
    return lax.dot_general(a, b, (((1,), (1,)), ((), ())), preferred_element_type=F32)


def _rms(x, g):
    return x * lax.rsqrt(jnp.mean(x * x, axis=-1, keepdims=True) + EPS) * g


def _split3(x):
    hi = x.astype(BF16)
    r1 = x - hi.astype(F32)
    mid = r1.astype(BF16)
    lo = (r1 - mid.astype(F32)).astype(BF16)
    return hi, mid, lo


def _const_spec(shape):
    nd = len(shape)
    return pl.BlockSpec(shape, lambda *_: (0,) * nd, pipeline_mode=pl.Buffered(1))


def _layer_spec(shape, layer):
    nd = len(shape)
    return pl.BlockSpec((None,) + tuple(shape), lambda *_: (layer,) + (0,) * nd,
                        pipeline_mode=pl.Buffered(1))


def _proj_kernel(x_ref, g1_ref, wfox_ref, wmain_ref, wsmall_ref, fb_ref, qng_ref, wuq_ref,
                 kvng_ref, wukv_ref,
                 cos_ref, sin_ref,
                 fq_ref, fk_ref, fvt_ref, dq1_ref, dq2_ref, dk1_ref, dk2_ref, dvt_ref,
                 mq_ref, mk_ref, mvt_ref, carry_ref, *, tm):
    i = pl.program_id(1)

    @pl.when(i == 0)
    def _():
        carry_ref[...] = jnp.zeros_like(carry_ref)

    xn = _rms(x_ref[0], g1_ref[...]).astype(BF16)

    def proj(w_ref, off, width):
        return _dot(xn, w_ref[:, off:off + width])

    def head(t, h):
        return t[:, h * HEAD_PAD:(h + 1) * HEAD_PAD]

    lane = lax.broadcasted_iota(jnp.int32, (tm, HEAD_PAD), 1)
    low = lane < QK_DIM
    extra = jnp.where(low, lane, lane - QK_DIM)
    in_half = (low, jnp.logical_not(low))
    ones3 = tuple(jnp.where(jnp.logical_not(m) & (extra < 3), 1.0, 0.0) for m in in_half)
    one1 = tuple(jnp.where(jnp.logical_not(m) & (extra == 0), 1.0, 0.0) for m in in_half)
    qk_scale = QK_DIM ** -0.5 * LOG2E

    def v_rows(t, half):
        t = t.T
        if half == 0:
            return t[:V_ROWS_ONES]
        return jnp.concatenate([t[QK_DIM:], t[:V_ROWS_ONES - QK_DIM]], axis=0)

    small = proj(wsmall_ref, 0, COLS_SMALL)
    z = small[:, OFF_FF:OFF_FF + LANES] + fb_ref[...]
    logf = jnp.minimum(z, 0.0) - jnp.log1p(jnp.exp(-jnp.abs(z)))
    rr = lax.broadcasted_iota(jnp.int32, (tm, tm), 0)
    cc = lax.broadcasted_iota(jnp.int32, (tm, tm), 1)
    tri = (cc <= rr).astype(BF16)
    parts = _dot(tri, jnp.concatenate(_split3(logf), axis=1))
    cum = (parts[:, :LANES] + parts[:, LANES:2 * LANES] + parts[:, 2 * LANES:]
           + carry_ref[...])
    carry_ref[...] = cum[tm - 1:tm, :]
    pr = lax.broadcasted_iota(jnp.int32, (3 * LANES, _GROUP), 0)
    pc = lax.broadcasted_iota(jnp.int32, (3 * LANES, _GROUP), 1)
    term, hd = pr // LANES, pr % LANES
    place = pc - hd * HEAD_PAD - jnp.where(hd % 2 == 0, QK_DIM, 0) == term
    fk_bias = _dot(jnp.concatenate(_split3(cum * (-LOG2E)), axis=1), place.astype(BF16))
    hq = proj(wfox_ref, OFF_FQ, _FOX)
    hk = proj(wfox_ref, OFF_FK, _FOX)
    hv = proj(wfox_ref, OFF_FV, _FOX)
    for h in range(HEADS):
        pair, half = divmod(h, 2)
        keep = in_half[half]
        fq_ref[0, h] = jnp.where(keep, head(hq, pair) * qk_scale, ones3[half]).astype(BF16)
        fk_ref[0, h] = jnp.where(keep, head(hk, pair), head(fk_bias, h)).astype(BF16)
        fvt_ref[0, h, 0] = v_rows(jnp.where(keep, head(hv, pair), one1[half]), half).astype(BF16)

    pos = (i * tm + lax.broadcasted_iota(jnp.int32, (tm, HEAD_PAD), 0)).astype(F32)
    hq = proj(wmain_ref, OFF_DQ, _GROUP)
    hk = proj(wmain_ref, OFF_DK, _GROUP)
    hv = proj(wmain_ref, OFF_DV, _GROUP)
    ones_rows = jnp.where(lax.broadcasted_iota(jnp.int32, (BF16_ROWS, tm), 0) == 0, 1.0, 0.0)
    for h in range(HEADS):
        slope = 2.0 ** (-8.0 * (h + 1) / HEADS)
        a_hi, a_mid, a_lo = _split3(pos * (slope * LOG2E))
        kbias = jnp.where(extra == 0, a_hi.astype(F32),
                          jnp.where(extra == 1, a_mid.astype(F32),
                                    jnp.where(extra == 2, a_lo.astype(F32), 0.0)))
        q = head(hq, h) * qk_scale
        k = head(hk, h)
        for half, (q_ref, k_ref) in enumerate(((dq1_ref, dk1_ref), (dq2_ref, dk2_ref))):
            q_ref[0, h] = jnp.where(in_half[half], q, ones3[half]).astype(BF16)
            k_ref[0, h] = jnp.where(in_half[half], k, kbias).astype(BF16)
        dvt_ref[0, h, 0] = jnp.concatenate([head(hv, h).T, ones_rows], axis=0).astype(BF16)

    cqn = _rms(proj(wmain_ref, OFF_CQ, MLA_Q_RANK), qng_ref[...]).astype(BF16)
    ckvn = _rms(proj(wmain_ref, OFF_CKV, MLA_KV_RANK), kvng_ref[...]).astype(BF16)
    cos = cos_ref[...]
    sin = sin_ref[...]
    krot = (small[:, OFF_KR:OFF_KR + LANES] * cos
            + small[:, OFF_KRP:OFF_KRP + LANES] * sin)
    qa = _dot(cqn, wuq_ref[:, :_GROUP])
    qb = _dot(cqn, wuq_ref[:, _GROUP:])
    kk = _dot(ckvn, wukv_ref[:, :_GROUP])
    vv = _dot(ckvn, wukv_ref[:, _GROUP:])
    mla_scale = (QK_DIM + MLA_ROPE) ** -0.5 * LOG2E
    for h in range(HEADS):
        mq_ref[0, h] = ((head(qa, h) * cos + head(qb, h) * sin) * mla_scale).astype(BF16)
        mk_ref[0, h] = (head(kk, h) + krot).astype(BF16)
        mvt_ref[0, h, 0] = v_rows(head(vv, h) + one1[0], 0).astype(BF16)


def _project(x, layer, g1, w_groups, fb, qng, wuq, kvng, wukv, cos_t, sin_t):
    b, s, _ = x.shape
    per_layer = functools.partial(_layer_spec, layer=layer)
    tm = TILE_PROJ
    rows = (jax.ShapeDtypeStruct((b, HEADS, s, HEAD_PAD), BF16),
            pl.BlockSpec((1, HEADS, tm, HEAD_PAD), lambda bi, i: (bi, 0, i, 0)))
    def cols(v_rows):
        return (jax.ShapeDtypeStruct((b, HEADS, s // tm, v_rows, tm), BF16),
                pl.BlockSpec((1, HEADS, 1, v_rows, tm), lambda bi, i: (bi, 0, i, 0, 0)))

    outs = [rows, rows, cols(V_ROWS_ONES), rows, rows, rows, rows, cols(V_ROWS_DIFF),
            rows, rows, cols(V_ROWS_ONES)]
    row_tab = pl.BlockSpec((tm, HEAD_PAD), lambda bi, i: (i, 0))
    return pl.pallas_call(
        functools.partial(_proj_kernel, tm=tm),
        grid=(b, s // tm),
        in_specs=[
            pl.BlockSpec((1, tm, D_MODEL), lambda bi, i: (bi, i, 0)),
            per_layer((1, D_MODEL)),
            per_layer((D_MODEL, COLS_FOX)),
            per_layer((D_MODEL, COLS_MAIN)),
            per_layer((D_MODEL, COLS_SMALL)),
            per_layer((1, LANES)),
            per_layer((1, MLA_Q_RANK)),
            per_layer((MLA_Q_RANK, 2 * _GROUP)),
            per_layer((1, MLA_KV_RANK)),
            per_layer((MLA_KV_RANK, 2 * _GROUP)),
            row_tab, row_tab,
        ],
        out_specs=[o[1] for o in outs],
        out_shape=[o[0] for o in outs],
        scratch_shapes=[pltpu.VMEM((1, LANES), F32)],
        compiler_params=pltpu.CompilerParams(
            dimension_semantics=("arbitrary", "arbitrary"), vmem_limit_bytes=VMEM_LIMIT),
        name="proj",
    )(x, g1, *w_groups, fb, qng, wuq, kvng, wukv, cos_t, sin_t)


def _tile_ids(t):
    key = lax.broadcasted_iota(jnp.int32, (t, t), 0)
    qry = lax.broadcasted_iota(jnp.int32, (t, t), 1)
    return key, qry


def _online_step(s, vt, m_ref, acc_ref, lanes=slice(None)):
    m_prev = m_ref[:, lanes]
    m_new = jnp.maximum(m_prev, jnp.max(s, axis=0, keepdims=True))
    p = jnp.exp2(s - m_new)
    alpha = jnp.exp2(m_prev - m_new)
    acc_ref[:, lanes] = alpha * acc_ref[:, lanes] + _dot(vt, p.astype(BF16))
    m_ref[:, lanes] = m_new


class _Map(NamedTuple):
    q_ref: object
    q_next_ref: object
    k_ref: object
    vt_ref: object
    bias_ref: object
    m_ref: object
    acc_ref: object


def _flash(i, maps, buf_a, buf_b, tk):
    for mp in maps:
        mp.m_ref[...] = jnp.full_like(mp.m_ref, MASKED)
        mp.acc_ref[...] = jnp.zeros_like(mp.acc_ref)

    def k_tile(mp, kt):
        return mp.k_ref[0, 0, pl.ds(pl.multiple_of(kt * tk, tk), tk), :]

    def compute_scores(kt, buf):
        for mp, s_ref in zip(maps, buf):
            s_ref[...] = _dot_nt(k_tile(mp, kt), mp.q_ref[0, 0])

    def consume(kt, buf):
        for mp, s_ref in zip(maps, buf):
            _online_step(s_ref[...], mp.vt_ref[0, 0, kt], mp.m_ref, mp.acc_ref)

    @pl.when(i == 0)
    def _():
        compute_scores(0, buf_a)

    def pair(j):
        compute_scores(j + 1, buf_b)
        consume(j, buf_a)
        compute_scores(j + 2, buf_a)
        consume(j + 1, buf_b)

    odd = jnp.bitwise_and(i, 1)

    @pl.when(odd == 1)
    def _():
        pair(0)

    def body(jj, c):
        j = 2 * odd + 4 * jj
        pair(j)
        pair(j + 2)
        return c

    lax.fori_loop(0, lax.shift_right_logical(i, 1), body, 0)

    lower, upper = slice(0, tk), slice(tk, None)
    for mp, sb_ref in zip(maps, buf_b):
        sb_ref[:, lower] = _dot_nt(k_tile(mp, 2 * i + 1), mp.q_ref[0, 0, upper, :])
    for mp, sa_ref, sb_ref in zip(maps, buf_a, buf_b):
        bias = mp.bias_ref[...]
        vt0, vt1 = mp.vt_ref[0, 0, 2 * i], mp.vt_ref[0, 0, 2 * i + 1]
        _online_step(sa_ref[:, lower] + bias, vt0, mp.m_ref, mp.acc_ref, lanes=lower)
        s = jnp.concatenate([sa_ref[:, upper], sb_ref[:, lower] + bias], axis=0)
        _online_step(s, jnp.concatenate([vt0, vt1], axis=1), mp.m_ref, mp.acc_ref, lanes=upper)

    for mp, sa_ref in zip(maps, buf_a):
        sa_ref[...] = _dot_nt(k_tile(mp, 0), mp.q_next_ref[0, 0])


def _normalized(acc_ref, rows):
    acc = acc_ref[...]
    return acc[:rows] * (1.0 / acc[rows:rows + 1, :])


def _diag_bias(tk, chunked, slope2=None):
    key, qry = _tile_ids(tk)
    keep = (key // CHUNK) <= (qry // CHUNK) if chunked else key <= qry
    if slope2 is None:
        return jnp.where(keep, 0.0, MASKED)
    return jnp.where(keep, jnp.maximum(key - qry, 0).astype(F32) * -slope2, MASKED)


def _fox_mla_kernel(fq_ref, fqn_ref, fk_ref, fvt_ref, mq_ref, mqn_ref, mk_ref, mvt_ref, o_ref,
                    sfa_ref, sma_ref, sfb_ref, smb_ref, fbias_ref, mbias_ref,
                    fm_ref, facc_ref, mm_ref, macc_ref, *, tk):
    i = pl.program_id(2)

    @pl.when(i == 0)
    def _():
        fbias_ref[...] = _diag_bias(tk, False)
        mbias_ref[...] = _diag_bias(tk, True)

    maps = (_Map(fq_ref, fqn_ref, fk_ref, fvt_ref, fbias_ref, fm_ref, facc_ref),
            _Map(mq_ref, mqn_ref, mk_ref, mvt_ref, mbias_ref, mm_ref, macc_ref))
    _flash(i, maps, (sfa_ref, sma_ref), (sfb_ref, smb_ref), tk)

    o_t = jnp.concatenate([_normalized(mp.acc_ref, QK_DIM) for mp in maps], axis=0)
    o_ref[0] = o_t.T.astype(BF16)


def _diff_kernel(q1_ref, q1n_ref, q2_ref, q2n_ref, k1_ref, k2_ref, vt_ref, lam_ref, g_ref, o_ref,
                 s1a_ref, s2a_ref, s1b_ref, s2b_ref, bias_ref,
                 m1_ref, acc1_ref, m2_ref, acc2_ref, *, tk, lam_init):
    h = pl.program_id(1)
    i = pl.program_id(2)

    @pl.when(i == 0)
    def _():
        slope2 = jnp.exp2(jnp.full((1, 1), -2.0, F32) * (h + 1).astype(F32)) * (2.0 * LOG2E)
        bias_ref[...] = _diag_bias(tk, True, slope2)

    maps = (_Map(q1_ref, q1n_ref, k1_ref, vt_ref, bias_ref, m1_ref, acc1_ref),
            _Map(q2_ref, q2n_ref, k2_ref, vt_ref, bias_ref, m2_ref, acc2_ref))
    _flash(i, maps, (s1a_ref, s2a_ref), (s1b_ref, s2b_ref), tk)

    lams = lam_ref[...]
    lam = (jnp.exp(jnp.sum(lams[0:1] * lams[1:2], axis=-1, keepdims=True))
           - jnp.exp(jnp.sum(lams[2:3] * lams[3:4], axis=-1, keepdims=True)) + lam_init)
    o_t = _normalized(acc1_ref, HEAD_PAD) - lam * _normalized(acc2_ref, HEAD_PAD)
    o_ref[0] = (_rms(o_t.T, g_ref[...]) * (1.0 - lam_init)).astype(BF16)


def _attn_specs(s, v_rows):
    tq, tk = TILE_Q, TILE_K
    last = s // tq - 1
    q_spec = pl.BlockSpec((1, 1, tq, HEAD_PAD), lambda b, h, i: (b, h, i, 0))
    q_next_spec = pl.BlockSpec((1, 1, tq, HEAD_PAD),
                               lambda b, h, i: (b, h, jnp.minimum(i + 1, last), 0))
    k_spec = pl.BlockSpec((1, 1, s, HEAD_PAD), lambda b, h, i: (b, h, 0, 0))
    vt_spec = pl.BlockSpec((1, 1, s // tk, v_rows, tk), lambda b, h, i: (b, h, 0, 0, 0))
    o_spec = pl.BlockSpec((1, tq, HEAD_PAD), lambda b, h, i: (b, i, h))
    return (q_spec, q_next_spec), k_spec, vt_spec, o_spec


def _fox_mla_attention(fq, fk, fvt, mq, mk, mvt):
    b, _, s, _ = fq.shape
    tq, tk = TILE_Q, TILE_K
    q_spec, k_spec, vt_spec, o_spec = _attn_specs(s, V_ROWS_ONES)
    score = pltpu.VMEM((tk, tq), F32)
    bias = pltpu.VMEM((tk, tk), F32)
    stat = pltpu.VMEM((1, tq), F32)
    acc = pltpu.VMEM((V_ROWS_ONES, tq), F32)
    return pl.pallas_call(
        functools.partial(_fox_mla_kernel, tk=tk),
        grid=(b, HEADS, s // tq),
        in_specs=[*q_spec, k_spec, vt_spec, *q_spec, k_spec, vt_spec],
        out_specs=o_spec,
        out_shape=jax.ShapeDtypeStruct((b, s, _GROUP), BF16),
        scratch_shapes=[score, score, score, score, bias, bias, stat, acc, stat, acc],
        compiler_params=pltpu.CompilerParams(
            dimension_semantics=("arbitrary",) * 3, vmem_limit_bytes=VMEM_LIMIT),
        name="fox_mla_attn",
    )(fq, fq, fk, fvt, mq, mq, mk, mvt)


def _diff_attention(q1, q2, k1, k2, vt, layer, lams, g, lam_init):
    b, _, s, _ = q1.shape
    tq, tk = TILE_Q, TILE_K
    q_spec, k_spec, vt_spec, o_spec = _attn_specs(s, V_ROWS_DIFF)
    score = pltpu.VMEM((tk, tq), F32)
    stat = pltpu.VMEM((1, tq), F32)
    acc = pltpu.VMEM((V_ROWS_DIFF, tq), F32)
    return pl.pallas_call(
        functools.partial(_diff_kernel, tk=tk, lam_init=lam_init),
        grid=(b, HEADS, s // tq),
        in_specs=[*q_spec, *q_spec, k_spec, k_spec, vt_spec,
                  _layer_spec((4, QK_DIM), layer), _layer_spec((1, HEAD_PAD), layer)],
        out_specs=o_spec,
        out_shape=jax.ShapeDtypeStruct((b, s, _GROUP), BF16),
        scratch_shapes=[score, score, score, score, pltpu.VMEM((tk, tk), F32),
                        stat, acc, stat, acc],
        compiler_params=pltpu.CompilerParams(
            dimension_semantics=("arbitrary",) * 3, vmem_limit_bytes=VMEM_LIMIT),
        name="diff_attn",
    )(q1, q1, q2, q2, k1, k2, vt, lams, g)


def _ffn_kernel(x_ref, ofm_ref, od_ref, wofm_ref, wod_ref, g2_ref,
                wup_ref, cw_ref, cb_ref, wdn_ref, gf_ref, o_ref, xn_ref, act_ref,
                *, tm, final):
    i = pl.program_id(1)

    @pl.when(i == 0)
    def _():
        xn_ref[0:HALO, :] = jnp.zeros((HALO, D_MODEL), BF16)

    @pl.when(i > 0)
    def _():
        xn_ref[0:HALO, :] = xn_ref[tm:tm + HALO, :]

    x1 = x_ref[0] + _dot(ofm_ref[0], wofm_ref[...]) + _dot(od_ref[0], wod_ref[...])
    xn_ref[HALO:HALO + tm, :] = _rms(x1, g2_ref[...]).astype(BF16)

    def conv(up, off):
        cw = cw_ref[:, off:off + FF_CHUNK]
        y = cb_ref[:, off:off + FF_CHUNK]
        for j in range(CONV_W):
            back = CONV_W - 1 - j
            tap = up if back == 0 else pltpu.roll(up, back, axis=0)
            y = y + tap[HALO:HALO + tm, :] * cw[j:j + 1, :]
        return y

    xn = xn_ref[...]
    for c in range(D_FF // FF_CHUNK):
        og = c * FF_CHUNK
        ou = D_FF + og
        yg = conv(_dot(xn, wup_ref[:, og:og + FF_CHUNK]), og)
        yu = conv(_dot(xn, wup_ref[:, ou:ou + FF_CHUNK]), ou)
        act = yg * (1.0 / (1.0 + jnp.exp(-yg))) * yu
        act_ref[:, og:og + FF_CHUNK] = act.astype(BF16)

    x2 = x1 + _dot(act_ref[...], wdn_ref[...])
    if final:
        x2 = _rms(x2, gf_ref[...])
    o_ref[0] = x2


def _ffn(x, o_fm, o_diff, layer, wo_fm, wo_d, g2, w_up, conv_w, conv_b, w_down, g_final, final):
    b, s, _ = x.shape
    per_layer = functools.partial(_layer_spec, layer=layer)
    tm = TILE_FFN
    row = lambda width: pl.BlockSpec((1, tm, width), lambda bi, i: (bi, i, 0))
    return pl.pallas_call(
        functools.partial(_ffn_kernel, tm=tm, final=final),
        grid=(b, s // tm),
        in_specs=[
            row(D_MODEL), row(_GROUP), row(_GROUP),
            per_layer((_GROUP, D_MODEL)), per_layer((_GROUP, D_MODEL)),
            per_layer((1, D_MODEL)),
            per_layer((D_MODEL, 2 * D_FF)),
            per_layer((CONV_W, 2 * D_FF)),
            per_layer((1, 2 * D_FF)),
            per_layer((D_FF, D_MODEL)),
            _const_spec((1, D_MODEL)),
        ],
        out_specs=row(D_MODEL),
        out_shape=jax.ShapeDtypeStruct((b, s, D_MODEL), F32),
        scratch_shapes=[pltpu.VMEM((HALO + tm, D_MODEL), BF16), pltpu.VMEM((tm, D_FF), BF16)],
        compiler_params=pltpu.CompilerParams(
            dimension_semantics=("arbitrary", "arbitrary"), vmem_limit_bytes=VMEM_LIMIT),
        name="ffn",
    )(x, o_fm, o_diff, wo_fm, wo_d, g2, w_up, conv_w, conv_b, w_down, g_final)


def _rot_half_cols(w):
    half = MLA_ROPE // 2
    return jnp.concatenate([w[..., half:], w[..., :half]], axis=-1)


def _split_w_in(w):
    n_main = COLS_FOX + HEADS
    n_kr = w.shape[-1] - MLA_ROPE
    zeros = lambda n: jnp.zeros(w.shape[:-1] + (n,), w.dtype)
    ff, kr = w[..., COLS_FOX:n_main], w[..., n_kr:]
    rope_pad = HEAD_PAD - QK_DIM - MLA_ROPE
    small = jnp.concatenate(
        [ff, zeros(LANES - HEADS),
         zeros(QK_DIM), kr, zeros(rope_pad),
         zeros(QK_DIM), _rot_half_cols(kr), zeros(rope_pad)], axis=-1)
    groups = (w[..., :COLS_FOX], w[..., n_main:n_kr], small)
    assert [g.shape[-1] for g in groups] == [COLS_FOX, COLS_MAIN, COLS_SMALL]
    return tuple(g.astype(BF16) for g in groups)


def _layout_w_uq(w):
    w = w.reshape(DEPTH, MLA_Q_RANK, HEADS, QK_DIM + MLA_ROPE)
    nope, rope = w[..., :QK_DIM], w[..., QK_DIM:]
    pad = jnp.zeros(w.shape[:-1] + (HEAD_PAD - QK_DIM - MLA_ROPE,), w.dtype)
    direct = jnp.concatenate([nope, rope, pad], axis=-1)
    swapped = jnp.concatenate([jnp.zeros_like(nope), _rot_half_cols(rope), pad], axis=-1)
    return jnp.concatenate([direct.reshape(DEPTH, MLA_Q_RANK, -1),
                            swapped.reshape(DEPTH, MLA_Q_RANK, -1)], axis=-1).astype(BF16)


def _layout_w_ukv(w):
    w = w.reshape(DEPTH, MLA_KV_RANK, HEADS, 2 * QK_DIM)
    pad = jnp.zeros(w.shape[:-1] + (HEAD_PAD - QK_DIM,), w.dtype)
    k = jnp.concatenate([w[..., :QK_DIM], pad], axis=-1).reshape(DEPTH, MLA_KV_RANK, -1)
    v = jnp.concatenate([w[..., QK_DIM:], pad], axis=-1).reshape(DEPTH, MLA_KV_RANK, -1)
    return jnp.concatenate([k, v], axis=-1).astype(BF16)


def _interleave_head_rows(w_fox, w_mla):
    parts = [w.reshape(DEPTH, HEADS, QK_DIM, -1) for w in (w_fox, w_mla)]
    return jnp.concatenate(parts, axis=2).reshape(DEPTH, HEADS * HEAD_PAD, -1)


def _rope_tables(s):
    half = MLA_ROPE // 2
    inv = 1.0 / (ROPE_THETA ** (jnp.arange(half, dtype=F32) / half))
    pos = (jnp.arange(s * half, dtype=jnp.int32) // half).astype(F32)
    ang = pos * jnp.tile(inv, s)
    cos, sin = jnp.cos(ang).reshape(s, half), jnp.sin(ang).reshape(s, half)
    tail = jnp.zeros((s, HEAD_PAD - QK_DIM - MLA_ROPE), F32)
    cos_t = jnp.concatenate([jnp.ones((s, QK_DIM), F32), cos, cos, tail], axis=1)
    sin_t = jnp.concatenate([jnp.zeros((s, QK_DIM), F32), -sin, sin, tail], axis=1)
    return cos_t, sin_t


def kernel(x, ln1_g, w_in, fgate_b, lam_q1, lam_k1, lam_q2, lam_k2, diff_norm_g, q_norm_g, w_uq,
           kv_norm_g, w_ukv, w_o, ln2_g, w_up, conv_w, conv_b, w_down, final_g):
    s = x.shape[1]
    cos_t, sin_t = _rope_tables(s)
    row = lambda v: v.reshape(DEPTH, 1, -1)
    w_groups = _split_w_in(w_in)
    fb = row(jnp.pad(fgate_b, ((0, 0), (0, LANES - HEADS))))
    wuq = _layout_w_uq(w_uq)
    wukv = _layout_w_ukv(w_ukv)
    lams = jnp.stack([lam_q1, lam_k1, lam_q2, lam_k2], axis=1)
    wo = w_o.astype(BF16)
    wo_fm = _interleave_head_rows(wo[:, :_FOX], wo[:, _FOX + _GROUP:])
    wo_d = wo[:, _FOX:_FOX + _GROUP]
    wup = w_up.astype(BF16)
    wdn = w_down.astype(BF16)
    g1, g2, qng, kvng, dng, cb = (row(v) for v in (ln1_g, ln2_g, q_norm_g, kv_norm_g,
                                                   diff_norm_g, conv_b))
    for i in range(DEPTH):
        lam_init = 0.8 - 0.6 * math.exp(-0.3 * i)
        heads = _project(x, i, g1, w_groups, fb, qng, wuq, kvng, wukv, cos_t, sin_t)
        fq, fk, fvt, dq1, dq2, dk1, dk2, dvt, mq, mk, mvt = heads
        o_fm = _fox_mla_attention(fq, fk, fvt, mq, mk, mvt)
        o_diff = _diff_attention(dq1, dq2, dk1, dk2, dvt, i, lams, dng, lam_init)
        x = _ffn(x, o_fm, o_diff, i, wo_fm, wo_d, g2, wup, conv_w, cb, wdn,
                 final_g.reshape(1, -1), final=(i == DEPTH - 1))
    return x
```
